```python
import math
import jax, jax.numpy as jnp
from jax import lax
import numpy as np

D_MODEL = 1024
BATCH = 8
SEQ = 4096
DEPTH = 1

D_RNN = D_MODEL
RNN_BLOCKS = 16
RNN_BLOCK = D_RNN // RNN_BLOCKS
CONV_WIDTH = 4
LRU_C = 8.0
N_HEADS = 8
HEAD_DIM = D_MODEL // (2 * N_HEADS)
V_DIM = 2 * HEAD_DIM
D_QK = N_HEADS * 2 * HEAD_DIM
D_ATTN = N_HEADS * V_DIM
Q_BLOCK = 128
SPLIT_SIZES = (D_RNN, D_RNN, D_QK, D_QK, D_ATTN, D_ATTN, 2 * D_MODEL)
D_IN_TOTAL = sum(SPLIT_SIZES)
NORM_EPS = 1e-6

kernel_name = "hawk_diffattn_gated_parallel_block"


def lambda_init(layer_idx):
    return 0.8 - 0.6 * math.exp(-0.3 * layer_idx)


def rmsnorm(x, g):
    xf = x.astype(jnp.float32)
    y = xf * lax.rsqrt(jnp.mean(xf * xf, axis=-1, keepdims=True) + NORM_EPS)
    return (y * g.astype(jnp.float32)).astype(x.dtype)


def causal_depthwise_conv(x, w, b):
    S = x.shape[1]
    xp = jnp.pad(x, ((0, 0), (CONV_WIDTH - 1, 0), (0, 0)))
    y = xp[:, 0:S] * w[0]
    for tap in range(1, CONV_WIDTH):
        y = y + xp[:, tap:tap + S] * w[tap]
    return y + b


def rg_lru(x, wa, ba, wx, bx, a_param):
    B, S, _ = x.shape
    xf = x.astype(jnp.float32)
    xb = xf.reshape(B, S, RNN_BLOCKS, RNN_BLOCK)
    r = jax.nn.sigmoid(jnp.einsum('bsgi,gij->bsgj', xb, wa.astype(jnp.float32)) + ba).reshape(B, S, D_RNN)
    i = jax.nn.sigmoid(jnp.einsum('bsgi,gij->bsgj', xb, wx.astype(jnp.float32)) + bx).reshape(B, S, D_RNN)
    log_a = -LRU_C * r * jax.nn.softplus(-a_param.astype(jnp.float32))
    a = jnp.exp(log_a)
    b = jnp.sqrt(-jnp.expm1(2.0 * log_a)) * (i * xf)

    def combine(left, right):
        a1, b1 = left
        a2, b2 = right
        return a1 * a2, a2 * b1 + b2

    _, h = lax.associative_scan(combine, (a, b), axis=1)
    return h.astype(x.dtype)


def diff_attention(q, k, v, lam):
    S = q.shape[1]
    scale = HEAD_DIM ** -0.5
    outs = []
    for qb in range(S // Q_BLOCK):
        q0 = qb * Q_BLOCK
        q1 = q0 + Q_BLOCK
        s = jnp.einsum('bqhcd,bkhcd->bhcqk', q[:, q0:q1], k[:, :q1]).astype(jnp.float32) * scale
        mask = (q0 + jnp.arange(Q_BLOCK))[:, None] >= jnp.arange(q1)[None, :]
        s = jnp.where(mask, s, -jnp.inf)
        p = jax.nn.softmax(s, axis=-1)
        w = p[:, :, 0] - lam * p[:, :, 1]
        outs.append(jnp.einsum('bhqk,bkhe->bqhe', w.astype(v.dtype), v[:, :q1]))
    return jnp.concatenate(outs, axis=1)


def setup_inputs(seed: int = 0) -> dict:
    key = jax.random.key(seed)
    ks = jax.random.split(key, 24)
    f32 = jnp.float32
    nrm = lambda k, shape, s: jax.random.normal(k, shape, f32) * s
    x = jax.random.normal(ks[0], (BATCH, SEQ, D_MODEL), f32)
    pre_g = 1.0 + nrm(ks[1], (DEPTH, D_MODEL), 0.02)
    post_g = 1.0 + nrm(ks[2], (DEPTH, D_MODEL), 0.02)
    w_in = nrm(ks[3], (DEPTH, D_MODEL, D_IN_TOTAL), D_MODEL ** -0.5)
    conv_w = nrm(ks[4], (DEPTH, CONV_WIDTH, D_RNN), CONV_WIDTH ** -0.5)
    conv_b = nrm(ks[5], (DEPTH, D_RNN), 0.01)
    lru_wa = nrm(ks[6], (DEPTH, RNN_BLOCKS, RNN_BLOCK, RNN_BLOCK), RNN_BLOCK ** -0.5)
    lru_ba = nrm(ks[7], (DEPTH, RNN_BLOCKS, RNN_BLOCK), 0.01)
    lru_wx = nrm(ks[8], (DEPTH, RNN_BLOCKS, RNN_BLOCK, RNN_BLOCK), RNN_BLOCK ** -0.5)
    lru_bx = nrm(ks[9], (DEPTH, RNN_BLOCKS, RNN_BLOCK), 0.01)
    r0 = jax.random.uniform(ks[10], (DEPTH, D_RNN), f32, 0.9, 0.999)
    s0 = r0 ** (1.0 / LRU_C)
    lru_a = jnp.log(s0) - jnp.log1p(-s0)
    attn_lq1 = nrm(ks[11], (DEPTH, HEAD_DIM), 0.1)
    attn_lk1 = nrm(ks[12], (DEPTH, HEAD_DIM), 0.1)
    attn_lq2 = nrm(ks[13], (DEPTH, HEAD_DIM), 0.1)
    attn_lk2 = nrm(ks[14], (DEPTH, HEAD_DIM), 0.1)
    subln_g = 1.0 + nrm(ks[15], (DEPTH, V_DIM), 0.02)
    w_br_rnn = nrm(ks[16], (DEPTH, D_RNN, D_MODEL), D_RNN ** -0.5)
    w_br_attn = nrm(ks[17], (DEPTH, D_ATTN, D_MODEL), D_ATTN ** -0.5)
    w_out = nrm(ks[18], (DEPTH, D_MODEL, D_MODEL), D_MODEL ** -0.5)
    return {"x": x, "pre_g": pre_g, "post_g": post_g, "w_in": w_in,
            "conv_w": conv_w, "conv_b": conv_b, "lru_wa": lru_wa, "lru_ba": lru_ba,
            "lru_wx": lru_wx, "lru_bx": lru_bx, "lru_a": lru_a,
            "attn_lq1": attn_lq1, "attn_lk1": attn_lk1, "attn_lq2": attn_lq2,
            "attn_lk2": attn_lk2, "subln_g": subln_g, "w_br_rnn": w_br_rnn,
            "w_br_attn": w_br_attn, "w_out": w_out}


def reference(x, pre_g, post_g, w_in, conv_w, conv_b, lru_wa, lru_ba, lru_wx, lru_bx,
              lru_a, attn_lq1, attn_lk1, attn_lq2, attn_lk2, subln_g, w_br_rnn,
              w_br_attn, w_out):
    B, S, _ = x.shape
    split_points = [int(v) for v in np.cumsum(SPLIT_SIZES)[:-1]]
    h = x
    for l in range(DEPTH):
        u = rmsnorm(h, pre_g[l])
        z = jnp.einsum('bsd,df->bsf', u, w_in[l])
        xr, zr, q, k, v, za, gm = jnp.split(z, split_points, axis=-1)

        xc = causal_depthwise_conv(xr, conv_w[l], conv_b[l])
        y_r = rg_lru(xc, lru_wa[l], lru_ba[l], lru_wx[l], lru_bx[l], lru_a[l]) * jax.nn.silu(zr)

        lam_init = lambda_init(l)
        lam = (jnp.exp(jnp.sum(attn_lq1[l] * attn_lk1[l]).astype(jnp.float32))
               - jnp.exp(jnp.sum(attn_lq2[l] * attn_lk2[l]).astype(jnp.float32)) + lam_init)
        qh = q.reshape(B, S, N_HEADS, 2, HEAD_DIM)
        kh = k.reshape(B, S, N_HEADS, 2, HEAD_DIM)
        vh = v.reshape(B, S, N_HEADS, V_DIM)
        o = diff_attention(qh, kh, vh, lam)
        o = rmsnorm(o, subln_g[l]) * (1.0 - lam_init)
        y_a = o.reshape(B, S, D_ATTN) * jax.nn.silu(za)

        g_r, g_a = jnp.split(jax.nn.sigmoid(gm), 2, axis=-1)
        m = (g_r * jnp.einsum('bse,ed->bsd', y_r, w_br_rnn[l])
             + g_a * jnp.einsum('bse,ed->bsd', y_a, w_br_attn[l]))
        y = jnp.einsum('bsd,de->bse', m, w_out[l])
        h = h + rmsnorm(y, post_g[l])
    return h
```

```python
import functools
import math

import jax
import jax.numpy as jnp
from jax import lax
from jax.experimental import pallas as pl
from jax.experimental.pallas import tpu as pltpu

F32 = jnp.float32
BF16 = jnp.bfloat16

D_MODEL = 1024
RNN_BLOCKS = 16
RNN_BLOCK = 64
CONV_WIDTH = 4
LRU_C = 8.0
N_HEADS = 8
HEAD_DIM = 64
V_DIM = 128
NORM_EPS = 1e-6
D_IN_TOTAL = 8192

LANES = 128
COL_XR, COL_ZR, COL_Q, COL_K, COL_V, COL_ZA, COL_GM = 0, 8, 16, 24, 32, 40, 48

VMEM_LIMIT = 56 * 1024 * 1024


def _sigmoid(x):
    return 0.5 * jnp.tanh(0.5 * x) + 0.5


def _inproj_kernel(x_ref, g_ref, w_ref, o_ref, u_ref):
    @pl.when(pl.program_id(1) == 0)
    def _():
        x = x_ref[...]
        ms = jnp.mean(x * x, axis=-1, keepdims=True)
        u_ref[...] = (x * lax.rsqrt(ms + NORM_EPS) * g_ref[...]).astype(BF16)

    o_ref[...] = jnp.dot(u_ref[...], w_ref[...],
                         preferred_element_type=F32).astype(BF16)


def _in_proj(x2, g, w, tm=1024, tn=1024):
    n_tok = x2.shape[0]
    return pl.pallas_call(
        _inproj_kernel,
        grid=(n_tok // tm, D_IN_TOTAL // tn),
        in_specs=[
            pl.BlockSpec((tm, D_MODEL), lambda i, j: (i, 0)),
            pl.BlockSpec((1, D_MODEL), lambda i, j: (0, 0)),
            pl.BlockSpec((D_MODEL, tn), lambda i, j: (0, j)),
        ],
        out_specs=pl.BlockSpec((tm, tn), lambda i, j: (i, j)),
        out_shape=jax.ShapeDtypeStruct((n_tok, D_IN_TOTAL), BF16),
        scratch_shapes=[pltpu.VMEM((tm, D_MODEL), BF16)],
        compiler_params=pltpu.CompilerParams(
            dimension_semantics=("arbitrary", "arbitrary"),
            vmem_limit_bytes=VMEM_LIMIT),
        name="in_proj",
    )(x2, g, w)


RNN_CB = 256
RNN_T = 256
RNN_SLABS = RNN_CB // LANES
PREV_ROWS = 8


def _rnn_kernel(xr_ref, zr_ref, cw_ref, cb_ref, wg_ref, bg_ref, la_ref, o_ref,
                xe_ref, a_ref, b_ref, h_ref):
    n_b = xr_ref.shape[0]
    t_blk = xr_ref.shape[1]

    @pl.when(pl.program_id(1) == 0)
    def _():
        xe_ref[:, 0:PREV_ROWS, :] = jnp.zeros((n_b, PREV_ROWS, RNN_CB), F32)
        h_ref[...] = jnp.zeros_like(h_ref)

    cw = cw_ref[...]
    cbias = cb_ref[...]
    bg = bg_ref[0]
    la = la_ref[...]
    log_a_scale = -LRU_C * (jnp.maximum(-la, 0.0)
                            + jnp.log1p(jnp.exp(-jnp.abs(la))))

    for b in range(n_b):
        xe_ref[b, PREV_ROWS:PREV_ROWS + t_blk, :] = xr_ref[b].astype(F32)
        xc = cbias + cw[CONV_WIDTH - 1:CONV_WIDTH, :] * xe_ref[b, PREV_ROWS:PREV_ROWS + t_blk, :]
        for tap in range(CONV_WIDTH - 1):
            off = PREV_ROWS - (CONV_WIDTH - 1) + tap
            xc = xc + cw[tap:tap + 1, :] * xe_ref[b, off:off + t_blk, :]
        xe_ref[b, 0:PREV_ROWS, :] = xe_ref[b, t_blk:t_blk + PREV_ROWS, :]

        gates = jnp.dot(xc.astype(BF16), wg_ref[0], preferred_element_type=F32) + bg
        r = _sigmoid(gates[:, :RNN_CB])
        i = _sigmoid(gates[:, RNN_CB:])
        a = jnp.exp(log_a_scale * r)
        bb = jnp.sqrt(1.0 - a * a) * (i * xc)
        for s in range(RNN_SLABS):
            a_ref[s, pl.ds(b, t_blk, stride=n_b), :] = a[:, s * LANES:(s + 1) * LANES]
            b_ref[s, pl.ds(b, t_blk, stride=n_b), :] = bb[:, s * LANES:(s + 1) * LANES]

    def scan_step(t, hs):
        row = pl.multiple_of(t * n_b, n_b)
        out = []
        for s in range(RNN_SLABS):
            h = a_ref[s, pl.ds(row, n_b), :] * hs[s] + b_ref[s, pl.ds(row, n_b), :]
            b_ref[s, pl.ds(row, n_b), :] = h
            out.append(h)
        return tuple(out)

    hs = lax.fori_loop(0, t_blk, scan_step,
                       tuple(h_ref[s] for s in range(RNN_SLABS)), unroll=8)
    for s in range(RNN_SLABS):
        h_ref[s] = hs[s]

    for b in range(n_b):
        h = jnp.concatenate(
            [b_ref[s, pl.ds(b, t_blk, stride=n_b), :] for s in range(RNN_SLABS)], axis=1)
        zr = zr_ref[b].astype(F32)
        o_ref[b] = (h * (zr * _sigmoid(zr))).astype(BF16)


def _rnn_branch(z3, conv_w, conv_b, wg, bg, lru_a):
    n_b, seq, _ = z3.shape
    n_c = D_MODEL // RNN_CB
    return pl.pallas_call(
        _rnn_kernel,
        grid=(n_c, seq // RNN_T),
        in_specs=[
            pl.BlockSpec((n_b, RNN_T, RNN_CB), lambda c, t: (0, t, c)),
            pl.BlockSpec((n_b, RNN_T, RNN_CB), lambda c, t: (0, t, n_c + c)),
            pl.BlockSpec((CONV_WIDTH, RNN_CB), lambda c, t: (0, c)),
            pl.BlockSpec((1, RNN_CB), lambda c, t: (0, c)),
            pl.BlockSpec((1, RNN_CB, 2 * RNN_CB), lambda c, t: (c, 0, 0)),
            pl.BlockSpec((1, 1, 2 * RNN_CB), lambda c, t: (c, 0, 0)),
            pl.BlockSpec((1, RNN_CB), lambda c, t: (0, c)),
        ],
        out_specs=pl.BlockSpec((n_b, RNN_T, RNN_CB), lambda c, t: (0, t, c)),
        out_shape=jax.ShapeDtypeStruct((n_b, seq, D_MODEL), BF16),
        scratch_shapes=[
            pltpu.VMEM((n_b, RNN_T + PREV_ROWS, RNN_CB), F32),
            pltpu.VMEM((RNN_SLABS, RNN_T * n_b, LANES), F32),
            pltpu.VMEM((RNN_SLABS, RNN_T * n_b, LANES), F32),
            pltpu.VMEM((RNN_SLABS, n_b, LANES), F32),
        ],
        compiler_params=pltpu.CompilerParams(
            dimension_semantics=("arbitrary", "arbitrary"),
            vmem_limit_bytes=VMEM_LIMIT),
        name="rnn",
    )(z3, z3, conv_w, conv_b, wg, bg, lru_a)


ATT_TQ = 256
ATT_TK = 256


def _attn_kernel(lq1_ref, lk1_ref, lq2_ref, lk2_ref, sg_ref, q_ref, k_ref, v_ref,
                 za_ref, o_ref, *, lam_init):
    tq, tk = ATT_TQ, ATT_TK
    qi = pl.program_id(2)

    lam = (jnp.exp(jnp.sum(lq1_ref[...] * lk1_ref[...], axis=-1, keepdims=True))
           - jnp.exp(jnp.sum(lq2_ref[...] * lk2_ref[...], axis=-1, keepdims=True))
           + lam_init)

    q = q_ref[0] * jnp.asarray(HEAD_DIM ** -0.5, BF16)
    lane = lax.broadcasted_iota(jnp.int32, (tq, 2 * HEAD_DIM), 1)
    zero = jnp.zeros_like(q)
    q2 = jnp.concatenate([jnp.where(lane < HEAD_DIM, q, zero),
                          jnp.where(lane >= HEAD_DIM, q, zero)], axis=0)

    def scores(j):
        start = pl.multiple_of(j * tk, tk)
        kt = k_ref[0, pl.ds(start, tk), :]
        s = lax.dot_general(q2, kt, (((1,), (1,)), ((), ())),
                            preferred_element_type=F32)
        return s, v_ref[0, pl.ds(start, tk), :]

    s, vt = scores(qi)
    row = lax.broadcasted_iota(jnp.int32, (2 * tq, tk), 0)
    col = lax.broadcasted_iota(jnp.int32, (2 * tq, tk), 1)
    row = jnp.where(row >= tq, row - tq, row)
    s = jnp.where(row >= col, s, -jnp.inf)
    m = jnp.max(s, axis=1, keepdims=True)
    p = jnp.exp(s - m)
    l = jnp.sum(p, axis=1, keepdims=True)
    acc = jnp.dot(p.astype(BF16), vt, preferred_element_type=F32)

    def body(j, carry):
        m, l, acc = carry
        s, vt = scores(j)
        m_new = jnp.maximum(m, jnp.max(s, axis=1, keepdims=True))
        alpha = jnp.exp(m - m_new)
        p = jnp.exp(s - m_new)
        l = alpha * l + jnp.sum(p, axis=1, keepdims=True)
        acc = alpha * acc + jnp.dot(p.astype(BF16), vt, preferred_element_type=F32)
        return m_new, l, acc

    m, l, acc = lax.fori_loop(0, qi, body, (m, l, acc))

    o = acc / l
    o = o[:tq] - lam * o[tq:]
    o = o * lax.rsqrt(jnp.mean(o * o, axis=-1, keepdims=True) + NORM_EPS)
    o = o * sg_ref[...] * (1.0 - lam_init)
    za = za_ref[0].astype(F32)
    o_ref[0] = (o * (za * _sigmoid(za))).astype(BF16)


def _attn_branch(z3, lq1, lk1, lq2, lk2, subln_g, lam_init):
    n_b, seq, _ = z3.shape
    small = lambda n: pl.BlockSpec((1, n), lambda b, h, i: (0, 0))
    return pl.pallas_call(
        functools.partial(_attn_kernel, lam_init=lam_init),
        grid=(n_b, N_HEADS, seq // ATT_TQ),
        in_specs=[
            small(HEAD_DIM), small(HEAD_DIM), small(HEAD_DIM), small(HEAD_DIM),
            small(V_DIM),
            pl.BlockSpec((1, ATT_TQ, LANES), lambda b, h, i: (b, i, COL_Q + h)),
            pl.BlockSpec((1, seq, LANES), lambda b, h, i: (b, 0, COL_K + h)),
            pl.BlockSpec((1, seq, LANES), lambda b, h, i: (b, 0, COL_V + h)),
            pl.BlockSpec((1, ATT_TQ, LANES), lambda b, h, i: (b, i, COL_ZA + h)),
        ],
        out_specs=pl.BlockSpec((1, ATT_TQ, LANES), lambda b, h, i: (b, i, h)),
        out_shape=jax.ShapeDtypeStruct((n_b, seq, N_HEADS * V_DIM), BF16),
        compiler_params=pltpu.CompilerParams(
            dimension_semantics=("arbitrary", "arbitrary", "arbitrary"),
            vmem_limit_bytes=VMEM_LIMIT),
        name="diff_attn",
    )(lq1, lk1, lq2, lk2, subln_g, z3, z3, z3, z3)


def _merge_kernel(x_ref, yr_ref, ya_ref, gm_ref, wr_ref, wa_ref, wo_ref, pg_ref, o_ref):
    pr = jnp.dot(yr_ref[...], wr_ref[...], preferred_element_type=F32)
    pa = jnp.dot(ya_ref[...], wa_ref[...], preferred_element_type=F32)
    g = _sigmoid(gm_ref[...].astype(F32))
    m = g[:, :D_MODEL] * pr + g[:, D_MODEL:] * pa
    y = jnp.dot(m.astype(BF16), wo_ref[...], preferred_element_type=F32)
    y = y * lax.rsqrt(jnp.mean(y * y, axis=-1, keepdims=True) + NORM_EPS)
    o_ref[...] = x_ref[...] + y * pg_ref[...]


def _merge(x2, yr, ya, z2, wr, wa, wo, post_g, tm=512):
    n_tok = x2.shape[0]
    full = lambda r, c: pl.BlockSpec((r, c), lambda i: (0, 0))
    return pl.pallas_call(
        _merge_kernel,
        grid=(n_tok // tm,),
        in_specs=[
            pl.BlockSpec((tm, D_MODEL), lambda i: (i, 0)),
            pl.BlockSpec((tm, D_MODEL), lambda i: (i, 0)),
            pl.BlockSpec((tm, D_MODEL), lambda i: (i, 0)),
            pl.BlockSpec((tm, 2 * D_MODEL), lambda i: (i, COL_GM * LANES // (2 * D_MODEL))),
            full(D_MODEL, D_MODEL), full(D_MODEL, D_MODEL), full(D_MODEL, D_MODEL),
            full(1, D_MODEL),
        ],
        out_specs=pl.BlockSpec((tm, D_MODEL), lambda i: (i, 0)),
        out_shape=jax.ShapeDtypeStruct((n_tok, D_MODEL), F32),
        compiler_params=pltpu.CompilerParams(
            dimension_semantics=("arbitrary",),
            vmem_limit_bytes=VMEM_LIMIT),
        name="merge",
    )(x2, yr, ya, z2, wr, wa, wo, post_g)


def _block_diag_gate_weights(wa, wx):
    per = RNN_CB // RNN_BLOCK

    def expand(w):
        w = w.reshape(D_MODEL // RNN_CB, per, RNN_BLOCK, RNN_BLOCK)
        eye = jnp.eye(per, dtype=w.dtype)
        return jnp.einsum('cgij,gh->cgihj', w, eye).reshape(
            D_MODEL // RNN_CB, RNN_CB, RNN_CB)

    return jnp.concatenate([expand(wa), expand(wx)], axis=-1)


def kernel(x, pre_g, post_g, w_in, conv_w, conv_b, lru_wa, lru_ba, lru_wx, lru_bx, lru_a,
           attn_lq1, attn_lk1, attn_lq2, attn_lk2, subln_g, w_br_rnn, w_br_attn, w_out):
    n_b, seq, d = x.shape
    depth = pre_g.shape[0]
    h2 = x.reshape(n_b * seq, d)
    n_c = D_MODEL // RNN_CB
    for l in range(depth):
        lam_init = 0.8 - 0.6 * math.exp(-0.3 * l)
        z2 = _in_proj(h2, pre_g[l][None, :], w_in[l].astype(BF16))
        z3 = z2.reshape(n_b, seq, D_IN_TOTAL)

        wg = _block_diag_gate_weights(lru_wa[l], lru_wx[l]).astype(BF16)
        bg = jnp.concatenate([lru_ba[l].reshape(n_c, 1, RNN_CB),
                              lru_bx[l].reshape(n_c, 1, RNN_CB)], axis=-1)
        y_r = _rnn_branch(z3, conv_w[l], conv_b[l][None, :], wg, bg, lru_a[l][None, :])

        y_a = _attn_branch(z3, attn_lq1[l][None, :], attn_lk1[l][None, :],
                           attn_lq2[l][None, :], attn_lk2[l][None, :],
                           subln_g[l][None, :], lam_init)

        h2 = _merge(h2, y_r.reshape(n_b * seq, d), y_a.reshape(n_b * seq, d), z2,
                    w_br_rnn[l].astype(BF16), w_br_attn[l].astype(BF16),
                    w_out[l].astype(BF16), post_g[l][None, :])
    return h2.reshape(n_b, seq, d)
```

```python
import functools
import math

import jax
import jax.numpy as jnp
import numpy as np
from jax import lax
from jax.experimental import pallas as pl
from jax.experimental.pallas import tpu as pltpu

F32 = jnp.float32
BF16 = jnp.bfloat16

D_MODEL = 1024
RNN_BLOCKS = 16
RNN_BLOCK = 64
CONV_WIDTH = 4
LRU_C = 8.0
N_HEADS = 8
HEAD_DIM = 64
V_DIM = 128
NORM_EPS = 1e-6
D_IN_TOTAL = 8192

LANES = 128
COL_XR, COL_ZR, COL_Q, COL_K, COL_V, COL_ZA, COL_GM = 0, 8, 16, 24, 32, 40, 48

VMEM_LIMIT = 56 * 1024 * 1024


def _sigmoid(x):
    return 0.5 * jnp.tanh(0.5 * x) + 0.5


def _inproj_kernel(x_ref, g_ref, w_ref, o_ref, u_ref):
    @pl.when(pl.program_id(1) == 0)
    def _():
        x = x_ref[...]
        ms = jnp.mean(x * x, axis=-1, keepdims=True)
        u_ref[...] = (x * lax.rsqrt(ms + NORM_EPS) * g_ref[...]).astype(BF16)

    o_ref[...] = jnp.dot(u_ref[...], w_ref[...],
                         preferred_element_type=F32).astype(BF16)


def _in_proj(x2, g, w, tm=1024, tn=1024):
    n_tok = x2.shape[0]
    return pl.pallas_call(
        _inproj_kernel,
        grid=(n_tok // tm, D_IN_TOTAL // tn),
        in_specs=[
            pl.BlockSpec((tm, D_MODEL), lambda i, j: (i, 0)),
            pl.BlockSpec((1, D_MODEL), lambda i, j: (0, 0)),
            pl.BlockSpec((D_MODEL, tn), lambda i, j: (0, j)),
        ],
        out_specs=pl.BlockSpec((tm, tn), lambda i, j: (i, j)),
        out_shape=jax.ShapeDtypeStruct((n_tok, D_IN_TOTAL), BF16),
        scratch_shapes=[pltpu.VMEM((tm, D_MODEL), BF16)],
        compiler_params=pltpu.CompilerParams(
            dimension_semantics=("arbitrary", "arbitrary"),
            vmem_limit_bytes=VMEM_LIMIT),
        name="in_proj",
    )(x2, g, w)


RNN_CB = 256
RNN_T = 256
RNN_SLABS = RNN_CB // LANES
PREV_ROWS = 8


def _rnn_kernel(xr_ref, zr_ref, cw_ref, cb_ref, wg_ref, bg_ref, la_ref, o_ref,
                xe_ref, a_ref, b_ref, h_ref):
    n_b = xr_ref.shape[0]
    t_blk = xr_ref.shape[1]

    @pl.when(pl.program_id(1) == 0)
    def _():
        xe_ref[:, 0:PREV_ROWS, :] = jnp.zeros((n_b, PREV_ROWS, RNN_CB), F32)
        h_ref[...] = jnp.zeros_like(h_ref)

    cw = cw_ref[...]
    cbias = cb_ref[...]
    bg = bg_ref[0]
    la = la_ref[...]
    log_a_scale = -LRU_C * (jnp.maximum(-la, 0.0)
                            + jnp.log1p(jnp.exp(-jnp.abs(la))))

    for b in range(n_b):
        xe_ref[b, PREV_ROWS:PREV_ROWS + t_blk, :] = xr_ref[b].astype(F32)
        xc = cbias + cw[CONV_WIDTH - 1:CONV_WIDTH, :] * xe_ref[b, PREV_ROWS:PREV_ROWS + t_blk, :]
        for tap in range(CONV_WIDTH - 1):
            off = PREV_ROWS - (CONV_WIDTH - 1) + tap
            xc = xc + cw[tap:tap + 1, :] * xe_ref[b, off:off + t_blk, :]
        xe_ref[b, 0:PREV_ROWS, :] = xe_ref[b, t_blk:t_blk + PREV_ROWS, :]

        gates = jnp.dot(xc.astype(BF16), wg_ref[0], preferred_element_type=F32) + bg
        r = _sigmoid(gates[:, :RNN_CB])
        i = _sigmoid(gates[:, RNN_CB:])
        a = jnp.exp(log_a_scale * r)
        bb = jnp.sqrt(1.0 - a * a) * (i * xc)
        for s in range(RNN_SLABS):
            a_ref[s, pl.ds(b, t_blk, stride=n_b), :] = a[:, s * LANES:(s + 1) * LANES]
            b_ref[s, pl.ds(b, t_blk, stride=n_b), :] = bb[:, s * LANES:(s + 1) * LANES]

    def scan_step(t, hs):
        row = pl.multiple_of(t * n_b, n_b)
        out = []
        for s in range(RNN_SLABS):
            h = a_ref[s, pl.ds(row, n_b), :] * hs[s] + b_ref[s, pl.ds(row, n_b), :]
            b_ref[s, pl.ds(row, n_b), :] = h
            out.append(h)
        return tuple(out)

    hs = lax.fori_loop(0, t_blk, scan_step,
                       tuple(h_ref[s] for s in range(RNN_SLABS)), unroll=8)
    for s in range(RNN_SLABS):
        h_ref[s] = hs[s]

    for b in range(n_b):
        h = jnp.concatenate(
            [b_ref[s, pl.ds(b, t_blk, stride=n_b), :] for s in range(RNN_SLABS)], axis=1)
        zr = zr_ref[b].astype(F32)
        o_ref[b] = (h * (zr * _sigmoid(zr))).astype(BF16)


def _rnn_branch(z3, conv_w, conv_b, wg, bg, lru_a):
    n_b, seq, _ = z3.shape
    n_c = D_MODEL // RNN_CB
    return pl.pallas_call(
        _rnn_kernel,
        grid=(n_c, seq // RNN_T),
        in_specs=[
            pl.BlockSpec((n_b, RNN_T, RNN_CB), lambda c, t: (0, t, c)),
            pl.BlockSpec((n_b, RNN_T, RNN_CB), lambda c, t: (0, t, n_c + c)),
            pl.BlockSpec((CONV_WIDTH, RNN_CB), lambda c, t: (0, c)),
            pl.BlockSpec((1, RNN_CB), lambda c, t: (0, c)),
            pl.BlockSpec((1, RNN_CB, 2 * RNN_CB), lambda c, t: (c, 0, 0)),
            pl.BlockSpec((1, 1, 2 * RNN_CB), lambda c, t: (c, 0, 0)),
            pl.BlockSpec((1, RNN_CB), lambda c, t: (0, c)),
        ],
        out_specs=pl.BlockSpec((n_b, RNN_T, RNN_CB), lambda c, t: (0, t, c)),
        out_shape=jax.ShapeDtypeStruct((n_b, seq, D_MODEL), BF16),
        scratch_shapes=[
            pltpu.VMEM((n_b, RNN_T + PREV_ROWS, RNN_CB), F32),
            pltpu.VMEM((RNN_SLABS, RNN_T * n_b, LANES), F32),
            pltpu.VMEM((RNN_SLABS, RNN_T * n_b, LANES), F32),
            pltpu.VMEM((RNN_SLABS, n_b, LANES), F32),
        ],
        compiler_params=pltpu.CompilerParams(
            dimension_semantics=("arbitrary", "arbitrary"),
            vmem_limit_bytes=VMEM_LIMIT),
        name="rnn",
    )(z3, z3, conv_w, conv_b, wg, bg, lru_a)


ATT_TQ = 512
ATT_KC = 256
ATT_NC = 2
ATT_VROWS = V_DIM + 16


def _attn_schedule(seq):
    tiles, pairs, firsts = [], [], []
    for t in range(seq // ATT_TQ):
        n_pairs = -(-((t + 1) * ATT_TQ // ATT_KC) // ATT_NC)
        for j, p in enumerate([n_pairs - 1] + list(range(n_pairs - 1))):
            tiles.append(t)
            pairs.append(p)
            firsts.append(1 if j == 0 else 0)
    return np.asarray(tiles + pairs + firsts, np.int32), len(tiles)


def _attn_kernel(sched_ref, lq1_ref, lk1_ref, lq2_ref, lk2_ref, sg_ref, q_ref, k_ref,
                 v_ref, za_ref, o_ref, q2_ref, vt_ref, s0_ref, s1_ref, p0_ref, p1_ref,
                 alpha0_ref, alpha1_ref, m_ref, acc_ref, *, lam_init, n_steps):
    tq, kc, nc = ATT_TQ, ATT_KC, ATT_NC
    n_tiles = q_ref.shape[1] // tq
    s_refs, p_refs, alpha_refs = (s0_ref, s1_ref), (p0_ref, p1_ref), (alpha0_ref, alpha1_ref)
    n_chunks = v_ref.shape[1] // kc

    lam = (jnp.exp(jnp.sum(lq1_ref[...] * lk1_ref[...], axis=-1, keepdims=True))
           - jnp.exp(jnp.sum(lq2_ref[...] * lk2_ref[...], axis=-1, keepdims=True))
           + lam_init)

    lane = lax.broadcasted_iota(jnp.int32, (tq, 2 * HEAD_DIM), 1)

    def prep_q(t, _):
        start = pl.multiple_of(t * tq, tq)
        q = q_ref[0, pl.ds(start, tq), :] * jnp.asarray(HEAD_DIM ** -0.5, BF16)
        zero = jnp.zeros_like(q)
        q2_ref[t, :, 0:tq] = jnp.where(lane < HEAD_DIM, q, zero).T
        q2_ref[t, :, tq:2 * tq] = jnp.where(lane >= HEAD_DIM, q, zero).T
        return 0
    lax.fori_loop(0, n_tiles, prep_q, 0)

    ones_rows = jnp.where(
        lax.broadcasted_iota(jnp.int32, (ATT_VROWS - V_DIM, kc), 0) == 0, 1.0, 0.0
    ).astype(BF16)

    def prep_v(c, _):
        start = pl.multiple_of(c * kc, kc)
        vt_ref[c, 0:V_DIM, :] = v_ref[0, pl.ds(start, kc), :].T
        vt_ref[c, V_DIM:ATT_VROWS, :] = ones_rows
        return 0
    lax.fori_loop(0, n_chunks, prep_v, 0)

    acc_ref[...] = jnp.zeros_like(acc_ref)

    def tile_of(i):
        return sched_ref[i]

    def pair_of(i):
        return sched_ref[n_steps + i]

    def qk_stage(i, slot):
        t, p = tile_of(i), pair_of(i)
        q2 = q2_ref[t]
        for u in range(nc):
            start = pl.multiple_of((p * nc + u) * kc, kc)
            kt = k_ref[0, pl.ds(start, kc), :]
            s_refs[slot][u] = jnp.dot(kt, q2, preferred_element_type=F32)

    def sm_stage(i, slot, first):
        ss = [s_refs[slot][u] for u in range(nc)]
        if first:
            row = lax.broadcasted_iota(jnp.int32, (kc, 2 * tq), 0)
            col = lax.broadcasted_iota(jnp.int32, (kc, 2 * tq), 1)
            diff = row - jnp.where(col >= tq, col - tq, col)
            t, p = tile_of(i), pair_of(i)
            ss = [jnp.where(diff <= t * tq - (p * nc + u) * kc, s, -jnp.inf)
                  for u, s in enumerate(ss)]
        m_blk = jnp.max(ss[0], axis=0, keepdims=True)
        for s in ss[1:]:
            m_blk = jnp.maximum(m_blk, jnp.max(s, axis=0, keepdims=True))
        if first:
            m_new = m_blk
            alpha_refs[slot][...] = jnp.zeros_like(m_blk)
        else:
            m_old = m_ref[...]
            m_new = jnp.maximum(m_old, m_blk)
            alpha_refs[slot][...] = jnp.exp(m_old - m_new)
        m_ref[...] = m_new
        for u in range(nc):
            p_refs[slot][u] = jnp.exp(ss[u] - m_new).astype(BF16)

    def pv_stage(i, slot):
        p = pair_of(i)
        pv = jnp.dot(vt_ref[p * nc], p_refs[slot][0], preferred_element_type=F32)
        for u in range(1, nc):
            pv = pv + jnp.dot(vt_ref[p * nc + u], p_refs[slot][u],
                              preferred_element_type=F32)
        acc = alpha_refs[slot][...] * acc_ref[...] + pv
        acc_ref[...] = acc
        return acc

    def finalize(t, acc):
        o2 = acc[0:V_DIM] * (1.0 / acc[V_DIM:V_DIM + 1])
        o = (o2[:, :tq] - lam * o2[:, tq:]).T
        o = o * lax.rsqrt(jnp.mean(o * o, axis=-1, keepdims=True) + NORM_EPS)
        o = o * sg_ref[...] * (1.0 - lam_init)
        start = pl.multiple_of(t * tq, tq)
        za = za_ref[0, pl.ds(start, tq), :].astype(F32)
        o_ref[0, pl.ds(start, tq), :] = (o * (za * _sigmoid(za))).astype(BF16)

    def iteration(i, slot, first):
        acc = pv_stage(i - 1, 1 - slot)
        if first:
            finalize(tile_of(i - 1), acc)
        sm_stage(i, slot, first)
        qk_stage(jnp.minimum(i + 1, n_steps - 1), 1 - slot)

    def iteration_of(i, slot):
        is_first = sched_ref[2 * n_steps + i]

        @pl.when(is_first == 1)
        def _():
            iteration(i, slot, True)

        @pl.when(is_first == 0)
        def _():
            iteration(i, slot, False)

    qk_stage(0, 0)
    sm_stage(0, 0, True)
    qk_stage(1, 1)

    def body(j, _):
        iteration_of(2 * j + 1, 1)
        iteration_of(2 * j + 2, 0)
        return 0
    lax.fori_loop(0, (n_steps - 1) // 2, body, 0)
    if (n_steps - 1) % 2:
        iteration_of(n_steps - 1, (n_steps - 1) % 2)

    acc = pv_stage(n_steps - 1, (n_steps - 1) % 2)
    finalize(n_tiles - 1, acc)


def _attn_branch(z3, lq1, lk1, lq2, lk2, subln_g, lam_init):
    n_b, seq, _ = z3.shape
    sched, n_steps = _attn_schedule(seq)
    small = lambda n: pl.BlockSpec((1, n), lambda b, h: (0, 0))
    head = lambda col: pl.BlockSpec((1, seq, LANES), lambda b, h: (b, 0, col + h))
    return pl.pallas_call(
        functools.partial(_attn_kernel, lam_init=lam_init, n_steps=n_steps),
        grid=(n_b, N_HEADS),
        in_specs=[
            pl.BlockSpec(memory_space=pltpu.SMEM),
            small(HEAD_DIM), small(HEAD_DIM), small(HEAD_DIM), small(HEAD_DIM),
            small(V_DIM),
            head(COL_Q), head(COL_K), head(COL_V), head(COL_ZA),
        ],
        out_specs=pl.BlockSpec((1, seq, LANES), lambda b, h: (b, 0, h)),
        out_shape=jax.ShapeDtypeStruct((n_b, seq, N_HEADS * V_DIM), BF16),
        scratch_shapes=[
            pltpu.VMEM((seq // ATT_TQ, LANES, 2 * ATT_TQ), BF16),
            pltpu.VMEM((seq // ATT_KC, ATT_VROWS, ATT_KC), BF16),
            pltpu.VMEM((ATT_NC, ATT_KC, 2 * ATT_TQ), F32),
            pltpu.VMEM((ATT_NC, ATT_KC, 2 * ATT_TQ), F32),
            pltpu.VMEM((ATT_NC, ATT_KC, 2 * ATT_TQ), BF16),
            pltpu.VMEM((ATT_NC, ATT_KC, 2 * ATT_TQ), BF16),
            pltpu.VMEM((1, 2 * ATT_TQ), F32),
            pltpu.VMEM((1, 2 * ATT_TQ), F32),
            pltpu.VMEM((1, 2 * ATT_TQ), F32),
            pltpu.VMEM((ATT_VROWS, 2 * ATT_TQ), F32),
        ],
        compiler_params=pltpu.CompilerParams(
            dimension_semantics=("arbitrary", "arbitrary"),
            vmem_limit_bytes=VMEM_LIMIT),
        name="diff_attn",
    )(jnp.asarray(sched), lq1, lk1, lq2, lk2, subln_g, z3, z3, z3, z3)


def _merge_kernel(x_ref, yr_ref, ya_ref, gm_ref, wr_ref, wa_ref, wo_ref, pg_ref, o_ref):
    pr = jnp.dot(yr_ref[...], wr_ref[...], preferred_element_type=F32)
    pa = jnp.dot(ya_ref[...], wa_ref[...], preferred_element_type=F32)
    g = _sigmoid(gm_ref[...].astype(F32))
    m = g[:, :D_MODEL] * pr + g[:, D_MODEL:] * pa
    y = jnp.dot(m.astype(BF16), wo_ref[...], preferred_element_type=F32)
    y = y * lax.rsqrt(jnp.mean(y * y, axis=-1, keepdims=True) + NORM_EPS)
    o_ref[...] = x_ref[...] + y * pg_ref[...]


def _merge(x2, yr, ya, z2, wr, wa, wo, post_g, tm=512):
    n_tok = x2.shape[0]
    full = lambda r, c: pl.BlockSpec((r, c), lambda i: (0, 0))
    return pl.pallas_call(
        _merge_kernel,
        grid=(n_tok // tm,),
        in_specs=[
            pl.BlockSpec((tm, D_MODEL), lambda i: (i, 0)),
            pl.BlockSpec((tm, D_MODEL), lambda i: (i, 0)),
            pl.BlockSpec((tm, D_MODEL), lambda i: (i, 0)),
            pl.BlockSpec((tm, 2 * D_MODEL), lambda i: (i, COL_GM * LANES // (2 * D_MODEL))),
            full(D_MODEL, D_MODEL), full(D_MODEL, D_MODEL), full(D_MODEL, D_MODEL),
            full(1, D_MODEL),
        ],
        out_specs=pl.BlockSpec((tm, D_MODEL), lambda i: (i, 0)),
        out_shape=jax.ShapeDtypeStruct((n_tok, D_MODEL), F32),
        compiler_params=pltpu.CompilerParams(
            dimension_semantics=("arbitrary",),
            vmem_limit_bytes=VMEM_LIMIT),
        name="merge",
    )(x2, yr, ya, z2, wr, wa, wo, post_g)


def _block_diag_gate_weights(wa, wx):
    per = RNN_CB // RNN_BLOCK

    def expand(w):
        w = w.reshape(D_MODEL // RNN_CB, per, RNN_BLOCK, RNN_BLOCK)
        eye = jnp.eye(per, dtype=w.dtype)
        return jnp.einsum('cgij,gh->cgihj', w, eye).reshape(
            D_MODEL // RNN_CB, RNN_CB, RNN_CB)

    return jnp.concatenate([expand(wa), expand(wx)], axis=-1)


def kernel(x, pre_g, post_g, w_in, conv_w, conv_b, lru_wa, lru_ba, lru_wx, lru_bx, lru_a,
           attn_lq1, attn_lk1, attn_lq2, attn_lk2, subln_g, w_br_rnn, w_br_attn, w_out):
    n_b, seq, d = x.shape
    depth = pre_g.shape[0]
    h2 = x.reshape(n_b * seq, d)
    n_c = D_MODEL // RNN_CB
    for l in range(depth):
        lam_init = 0.8 - 0.6 * math.exp(-0.3 * l)
        z2 = _in_proj(h2, pre_g[l][None, :], w_in[l].astype(BF16))
        z3 = z2.reshape(n_b, seq, D_IN_TOTAL)

        wg = _block_diag_gate_weights(lru_wa[l], lru_wx[l]).astype(BF16)
        bg = jnp.concatenate([lru_ba[l].reshape(n_c, 1, RNN_CB),
                              lru_bx[l].reshape(n_c, 1, RNN_CB)], axis=-1)
        y_r = _rnn_branch(z3, conv_w[l], conv_b[l][None, :], wg, bg, lru_a[l][None, :])

        y_a = _attn_branch(z3, attn_lq1[l][None, :], attn_lk1[l][None, :],
                           attn_lq2[l][None, :], attn_lk2[l][None, :],
                           subln_g[l][None, :], lam_init)

        h2 = _merge(h2, y_r.reshape(n_b * seq, d), y_a.reshape(n_b * seq, d), z2,
                    w_br_rnn[l].astype(BF16), w_br_attn[l].astype(BF16),
                    w_out[l].astype(BF16), post_g[l][None, :])
    return h2.reshape(n_b, seq, d)
```

```python
import functools
import math

import jax
import jax.numpy as jnp
import numpy as np
from jax import lax
from jax.experimental import pallas as pl
from jax.experimental.pallas import tpu as pltpu

F32 = jnp.float32
BF16 = jnp.bfloat16

D_MODEL = 1024
RNN_BLOCKS = 16
RNN_BLOCK = 64
CONV_WIDTH = 4
LRU_C = 8.0
N_HEADS = 8
HEAD_DIM = 64
V_DIM = 128
NORM_EPS = 1e-6
D_IN_TOTAL = 8192

LANES = 128
COL_XR, COL_ZR, COL_Q, COL_K, COL_V, COL_ZA, COL_GM = 0, 8, 16, 24, 32, 40, 48

VMEM_LIMIT = 56 * 1024 * 1024


def _sigmoid(x):
    return 0.5 * jnp.tanh(0.5 * x) + 0.5


def _inproj_kernel(x_ref, g_ref, w_ref, o_ref, u_ref):
    @pl.when(pl.program_id(1) == 0)
    def _():
        x = x_ref[...]
        ms = jnp.mean(x * x, axis=-1, keepdims=True)
        u_ref[...] = (x * lax.rsqrt(ms + NORM_EPS) * g_ref[...]).astype(BF16)

    o_ref[...] = jnp.dot(u_ref[...], w_ref[...],
                         preferred_element_type=F32).astype(BF16)


def _in_proj(x2, g, w, tm=1024, tn=1024):
    n_tok = x2.shape[0]
    return pl.pallas_call(
        _inproj_kernel,
        grid=(n_tok // tm, D_IN_TOTAL // tn),
        in_specs=[
            pl.BlockSpec((tm, D_MODEL), lambda i, j: (i, 0)),
            pl.BlockSpec((1, D_MODEL), lambda i, j: (0, 0)),
            pl.BlockSpec((D_MODEL, tn), lambda i, j: (0, j)),
        ],
        out_specs=pl.BlockSpec((tm, tn), lambda i, j: (i, j)),
        out_shape=jax.ShapeDtypeStruct((n_tok, D_IN_TOTAL), BF16),
        scratch_shapes=[pltpu.VMEM((tm, D_MODEL), BF16)],
        compiler_params=pltpu.CompilerParams(
            dimension_semantics=("arbitrary", "arbitrary"),
            vmem_limit_bytes=VMEM_LIMIT),
        name="in_proj",
    )(x2, g, w)


RNN_CB = 256
RNN_T = 256
RNN_SLABS = RNN_CB // LANES
PREV_ROWS = 8


def _rnn_kernel(xr_ref, zr_ref, cw_ref, cb_ref, wg_ref, bg_ref, la_ref, o_ref,
                xe_ref, a_ref, b_ref, h_ref):
    n_b = xr_ref.shape[0]
    t_blk = xr_ref.shape[1]

    @pl.when(pl.program_id(1) == 0)
    def _():
        xe_ref[:, 0:PREV_ROWS, :] = jnp.zeros((n_b, PREV_ROWS, RNN_CB), F32)
        h_ref[...] = jnp.zeros_like(h_ref)

    cw = cw_ref[...]
    cbias = cb_ref[...]
    bg = bg_ref[0]
    la = la_ref[...]
    log_a_scale = -LRU_C * (jnp.maximum(-la, 0.0)
                            + jnp.log1p(jnp.exp(-jnp.abs(la))))

    for b in range(n_b):
        xe_ref[b, PREV_ROWS:PREV_ROWS + t_blk, :] = xr_ref[b].astype(F32)
        xc = cbias + cw[CONV_WIDTH - 1:CONV_WIDTH, :] * xe_ref[b, PREV_ROWS:PREV_ROWS + t_blk, :]
        for tap in range(CONV_WIDTH - 1):
            off = PREV_ROWS - (CONV_WIDTH - 1) + tap
            xc = xc + cw[tap:tap + 1, :] * xe_ref[b, off:off + t_blk, :]
        xe_ref[b, 0:PREV_ROWS, :] = xe_ref[b, t_blk:t_blk + PREV_ROWS, :]

        gates = jnp.dot(xc.astype(BF16), wg_ref[0], preferred_element_type=F32) + bg
        r = _sigmoid(gates[:, :RNN_CB])
        i = _sigmoid(gates[:, RNN_CB:])
        a = jnp.exp(log_a_scale * r)
        bb = jnp.sqrt(1.0 - a * a) * (i * xc)
        for s in range(RNN_SLABS):
            a_ref[s, pl.ds(b, t_blk, stride=n_b), :] = a[:, s * LANES:(s + 1) * LANES]
            b_ref[s, pl.ds(b, t_blk, stride=n_b), :] = bb[:, s * LANES:(s + 1) * LANES]

    def scan_step(t, hs):
        row = pl.multiple_of(t * n_b, n_b)
        out = []
        for s in range(RNN_SLABS):
            h = a_ref[s, pl.ds(row, n_b), :] * hs[s] + b_ref[s, pl.ds(row, n_b), :]
            b_ref[s, pl.ds(row, n_b), :] = h
            out.append(h)
        return tuple(out)

    hs = lax.fori_loop(0, t_blk, scan_step,
                       tuple(h_ref[s] for s in range(RNN_SLABS)), unroll=8)
    for s in range(RNN_SLABS):
        h_ref[s] = hs[s]

    for b in range(n_b):
        h = jnp.concatenate(
            [b_ref[s, pl.ds(b, t_blk, stride=n_b), :] for s in range(RNN_SLABS)], axis=1)
        zr = zr_ref[b].astype(F32)
        o_ref[b] = (h * (zr * _sigmoid(zr))).astype(BF16)


def _rnn_branch(z3, conv_w, conv_b, wg, bg, lru_a):
    n_b, seq, _ = z3.shape
    n_c = D_MODEL // RNN_CB
    return pl.pallas_call(
        _rnn_kernel,
        grid=(n_c, seq // RNN_T),
        in_specs=[
            pl.BlockSpec((n_b, RNN_T, RNN_CB), lambda c, t: (0, t, c)),
            pl.BlockSpec((n_b, RNN_T, RNN_CB), lambda c, t: (0, t, n_c + c)),
            pl.BlockSpec((CONV_WIDTH, RNN_CB), lambda c, t: (0, c)),
            pl.BlockSpec((1, RNN_CB), lambda c, t: (0, c)),
            pl.BlockSpec((1, RNN_CB, 2 * RNN_CB), lambda c, t: (c, 0, 0)),
            pl.BlockSpec((1, 1, 2 * RNN_CB), lambda c, t: (c, 0, 0)),
            pl.BlockSpec((1, RNN_CB), lambda c, t: (0, c)),
        ],
        out_specs=pl.BlockSpec((n_b, RNN_T, RNN_CB), lambda c, t: (0, t, c)),
        out_shape=jax.ShapeDtypeStruct((n_b, seq, D_MODEL), BF16),
        scratch_shapes=[
            pltpu.VMEM((n_b, RNN_T + PREV_ROWS, RNN_CB), F32),
            pltpu.VMEM((RNN_SLABS, RNN_T * n_b, LANES), F32),
            pltpu.VMEM((RNN_SLABS, RNN_T * n_b, LANES), F32),
            pltpu.VMEM((RNN_SLABS, n_b, LANES), F32),
        ],
        compiler_params=pltpu.CompilerParams(
            dimension_semantics=("arbitrary", "arbitrary"),
            vmem_limit_bytes=VMEM_LIMIT),
        name="rnn",
    )(z3, z3, conv_w, conv_b, wg, bg, lru_a)


ATT_TQ = 512
ATT_KC = 256
ATT_NC = 2
ATT_VROWS = V_DIM + 16


def _attn_schedule(seq):
    tiles, pairs, firsts = [], [], []
    for t in range(seq // ATT_TQ):
        n_pairs = -(-((t + 1) * ATT_TQ // ATT_KC) // ATT_NC)
        for j, p in enumerate([n_pairs - 1] + list(range(n_pairs - 1))):
            tiles.append(t)
            pairs.append(p)
            firsts.append(1 if j == 0 else 0)
    return np.asarray(tiles + pairs + firsts, np.int32), len(tiles)


def _attn_kernel(sched_ref, lq1_ref, lk1_ref, lq2_ref, lk2_ref, sg_ref, q_ref, k_ref,
                 v_ref, za_ref, o_ref, q2_ref, vt_ref, s0_ref, s1_ref, p0_ref, p1_ref,
                 alpha0_ref, alpha1_ref, mp0_ref, mp1_ref, m_ref, acc_ref, *, lam_init,
                 n_steps):
    tq, kc, nc = ATT_TQ, ATT_KC, ATT_NC
    n_tiles = q_ref.shape[1] // tq
    s_refs, p_refs, alpha_refs = (s0_ref, s1_ref), (p0_ref, p1_ref), (alpha0_ref, alpha1_ref)
    mp_refs = (mp0_ref, mp1_ref)
    n_chunks = v_ref.shape[1] // kc

    lam = (jnp.exp(jnp.sum(lq1_ref[...] * lk1_ref[...], axis=-1, keepdims=True))
           - jnp.exp(jnp.sum(lq2_ref[...] * lk2_ref[...], axis=-1, keepdims=True))
           + lam_init)

    lane = lax.broadcasted_iota(jnp.int32, (tq, 2 * HEAD_DIM), 1)

    def prep_q(t, _):
        start = pl.multiple_of(t * tq, tq)
        q = q_ref[0, pl.ds(start, tq), :]
        zero = jnp.zeros_like(q)
        q2_ref[t, :, 0:tq] = jnp.where(lane < HEAD_DIM, q, zero).T
        q2_ref[t, :, tq:2 * tq] = jnp.where(lane >= HEAD_DIM, q, zero).T
        return 0
    lax.fori_loop(0, n_tiles, prep_q, 0)

    ones_rows = jnp.where(
        lax.broadcasted_iota(jnp.int32, (ATT_VROWS - V_DIM, kc), 0) == 0, 1.0, 0.0
    ).astype(BF16)

    def prep_v(c, _):
        start = pl.multiple_of(c * kc, kc)
        vt_ref[c, 0:V_DIM, :] = v_ref[0, pl.ds(start, kc), :].T
        vt_ref[c, V_DIM:ATT_VROWS, :] = ones_rows
        return 0
    lax.fori_loop(0, n_chunks, prep_v, 0)

    acc_ref[...] = jnp.zeros_like(acc_ref)

    def tile_of(i):
        return sched_ref[i]

    def pair_of(i):
        return sched_ref[n_steps + i]

    def qk_stage(i, slot):
        t, p = tile_of(i), pair_of(i)
        q2 = q2_ref[t]
        for u in range(nc):
            start = pl.multiple_of((p * nc + u) * kc, kc)
            kt = k_ref[0, pl.ds(start, kc), :]
            s = jnp.dot(kt, q2, preferred_element_type=F32)
            s_refs[slot][u] = s
            mp_refs[slot][u] = jnp.max(s, axis=0, keepdims=True)

    def sm_stage(i, slot, first):
        ss = [s_refs[slot][u] for u in range(nc)]
        if first:
            row = lax.broadcasted_iota(jnp.int32, (kc, 2 * tq), 0)
            col = lax.broadcasted_iota(jnp.int32, (kc, 2 * tq), 1)
            diff = row - jnp.where(col >= tq, col - tq, col)
            t, p = tile_of(i), pair_of(i)
            ss = [jnp.where(diff <= t * tq - (p * nc + u) * kc, s, -jnp.inf)
                  for u, s in enumerate(ss)]
        if first:
            mps = [jnp.max(s, axis=0, keepdims=True) for s in ss]
        else:
            mps = [mp_refs[slot][u] for u in range(nc)]
        m_blk = functools.reduce(jnp.maximum, mps)
        if first:
            m_new = m_blk
            alpha_refs[slot][...] = jnp.zeros_like(m_blk)
        else:
            m_old = m_ref[...]
            m_new = jnp.maximum(m_old, m_blk)
            alpha_refs[slot][...] = jnp.exp2(m_old - m_new)
        m_ref[...] = m_new
        for u in range(nc):
            p_refs[slot][u] = jnp.exp2(ss[u] - m_new).astype(BF16)

    def pv_stage(i, slot):
        p = pair_of(i)
        pv = jnp.dot(vt_ref[p * nc], p_refs[slot][0], preferred_element_type=F32)
        for u in range(1, nc):
            pv = pv + jnp.dot(vt_ref[p * nc + u], p_refs[slot][u],
                              preferred_element_type=F32)
        acc = alpha_refs[slot][...] * acc_ref[...] + pv
        acc_ref[...] = acc
        return acc

    def finalize(t, acc):
        o2 = acc[0:V_DIM] * (1.0 / acc[V_DIM:V_DIM + 1])
        o = (o2[:, :tq] - lam * o2[:, tq:]).T
        o = o * lax.rsqrt(jnp.mean(o * o, axis=-1, keepdims=True) + NORM_EPS)
        o = o * sg_ref[...] * (1.0 - lam_init)
        start = pl.multiple_of(t * tq, tq)
        za = za_ref[0, pl.ds(start, tq), :].astype(F32)
        o_ref[0, pl.ds(start, tq), :] = (o * (za * _sigmoid(za))).astype(BF16)

    def iteration(i, slot, first):
        acc = pv_stage(i - 1, 1 - slot)
        if first:
            finalize(tile_of(i - 1), acc)
        sm_stage(i, slot, first)
        qk_stage(jnp.minimum(i + 1, n_steps - 1), 1 - slot)

    def iteration_of(i, slot):
        is_first = sched_ref[2 * n_steps + i]

        @pl.when(is_first == 1)
        def _():
            iteration(i, slot, True)

        @pl.when(is_first == 0)
        def _():
            iteration(i, slot, False)

    qk_stage(0, 0)
    sm_stage(0, 0, True)
    qk_stage(1, 1)

    def body(j, _):
        iteration_of(2 * j + 1, 1)
        iteration_of(2 * j + 2, 0)
        return 0
    lax.fori_loop(0, (n_steps - 1) // 2, body, 0)
    if (n_steps - 1) % 2:
        iteration_of(n_steps - 1, (n_steps - 1) % 2)

    acc = pv_stage(n_steps - 1, (n_steps - 1) % 2)
    finalize(n_tiles - 1, acc)


def _attn_branch(z3, lq1, lk1, lq2, lk2, subln_g, lam_init):
    n_b, seq, _ = z3.shape
    sched, n_steps = _attn_schedule(seq)
    small = lambda n: pl.BlockSpec((1, n), lambda b, h: (0, 0))
    head = lambda col: pl.BlockSpec((1, seq, LANES), lambda b, h: (b, 0, col + h))
    return pl.pallas_call(
        functools.partial(_attn_kernel, lam_init=lam_init, n_steps=n_steps),
        grid=(n_b, N_HEADS),
        in_specs=[
            pl.BlockSpec(memory_space=pltpu.SMEM),
            small(HEAD_DIM), small(HEAD_DIM), small(HEAD_DIM), small(HEAD_DIM),
            small(V_DIM),
            head(COL_Q), head(COL_K), head(COL_V), head(COL_ZA),
        ],
        out_specs=pl.BlockSpec((1, seq, LANES), lambda b, h: (b, 0, h)),
        out_shape=jax.ShapeDtypeStruct((n_b, seq, N_HEADS * V_DIM), BF16),
        scratch_shapes=[
            pltpu.VMEM((seq // ATT_TQ, LANES, 2 * ATT_TQ), BF16),
            pltpu.VMEM((seq // ATT_KC, ATT_VROWS, ATT_KC), BF16),
            pltpu.VMEM((ATT_NC, ATT_KC, 2 * ATT_TQ), F32),
            pltpu.VMEM((ATT_NC, ATT_KC, 2 * ATT_TQ), F32),
            pltpu.VMEM((ATT_NC, ATT_KC, 2 * ATT_TQ), BF16),
            pltpu.VMEM((ATT_NC, ATT_KC, 2 * ATT_TQ), BF16),
            pltpu.VMEM((1, 2 * ATT_TQ), F32),
            pltpu.VMEM((1, 2 * ATT_TQ), F32),
            pltpu.VMEM((ATT_NC, 1, 2 * ATT_TQ), F32),
            pltpu.VMEM((ATT_NC, 1, 2 * ATT_TQ), F32),
            pltpu.VMEM((1, 2 * ATT_TQ), F32),
            pltpu.VMEM((ATT_VROWS, 2 * ATT_TQ), F32),
        ],
        compiler_params=pltpu.CompilerParams(
            dimension_semantics=("arbitrary", "arbitrary"),
            vmem_limit_bytes=VMEM_LIMIT),
        name="diff_attn",
    )(jnp.asarray(sched), lq1, lk1, lq2, lk2, subln_g, z3, z3, z3, z3)


def _merge_kernel(x_ref, yr_ref, ya_ref, gm_ref, wr_ref, wa_ref, wo_ref, pg_ref, o_ref):
    pr = jnp.dot(yr_ref[...], wr_ref[...], preferred_element_type=F32)
    pa = jnp.dot(ya_ref[...], wa_ref[...], preferred_element_type=F32)
    g = _sigmoid(gm_ref[...].astype(F32))
    m = g[:, :D_MODEL] * pr + g[:, D_MODEL:] * pa
    y = jnp.dot(m.astype(BF16), wo_ref[...], preferred_element_type=F32)
    y = y * lax.rsqrt(jnp.mean(y * y, axis=-1, keepdims=True) + NORM_EPS)
    o_ref[...] = x_ref[...] + y * pg_ref[...]


def _merge(x2, yr, ya, z2, wr, wa, wo, post_g, tm=512):
    n_tok = x2.shape[0]
    full = lambda r, c: pl.BlockSpec((r, c), lambda i: (0, 0))
    return pl.pallas_call(
        _merge_kernel,
        grid=(n_tok // tm,),
        in_specs=[
            pl.BlockSpec((tm, D_MODEL), lambda i: (i, 0)),
            pl.BlockSpec((tm, D_MODEL), lambda i: (i, 0)),
            pl.BlockSpec((tm, D_MODEL), lambda i: (i, 0)),
            pl.BlockSpec((tm, 2 * D_MODEL), lambda i: (i, COL_GM * LANES // (2 * D_MODEL))),
            full(D_MODEL, D_MODEL), full(D_MODEL, D_MODEL), full(D_MODEL, D_MODEL),
            full(1, D_MODEL),
        ],
        out_specs=pl.BlockSpec((tm, D_MODEL), lambda i: (i, 0)),
        out_shape=jax.ShapeDtypeStruct((n_tok, D_MODEL), F32),
        compiler_params=pltpu.CompilerParams(
            dimension_semantics=("arbitrary",),
            vmem_limit_bytes=VMEM_LIMIT),
        name="merge",
    )(x2, yr, ya, z2, wr, wa, wo, post_g)


def _q_column_scale():
    col = np.ones((1, D_IN_TOTAL), np.float32)
    col[:, COL_Q * LANES:COL_K * LANES] = HEAD_DIM ** -0.5 * math.log2(math.e)
    return jnp.asarray(col)


def _block_diag_gate_weights(wa, wx):
    per = RNN_CB // RNN_BLOCK

    def expand(w):
        w = w.reshape(D_MODEL // RNN_CB, per, RNN_BLOCK, RNN_BLOCK)
        eye = jnp.eye(per, dtype=w.dtype)
        return jnp.einsum('cgij,gh->cgihj', w, eye).reshape(
            D_MODEL // RNN_CB, RNN_CB, RNN_CB)

    return jnp.concatenate([expand(wa), expand(wx)], axis=-1)


def kernel(x, pre_g, post_g, w_in, conv_w, conv_b, lru_wa, lru_ba, lru_wx, lru_bx, lru_a,
           attn_lq1, attn_lk1, attn_lq2, attn_lk2, subln_g, w_br_rnn, w_br_attn, w_out):
    n_b, seq, d = x.shape
    depth = pre_g.shape[0]
    h2 = x.reshape(n_b * seq, d)
    n_c = D_MODEL // RNN_CB
    for l in range(depth):
        lam_init = 0.8 - 0.6 * math.exp(-0.3 * l)
        z2 = _in_proj(h2, pre_g[l][None, :], (w_in[l] * _q_column_scale()).astype(BF16))
        z3 = z2.reshape(n_b, seq, D_IN_TOTAL)

        wg = _block_diag_gate_weights(lru_wa[l], lru_wx[l]).astype(BF16)
        bg = jnp.concatenate([lru_ba[l].reshape(n_c, 1, RNN_CB),
                              lru_bx[l].reshape(n_c, 1, RNN_CB)], axis=-1)
        y_r = _rnn_branch(z3, conv_w[l], conv_b[l][None, :], wg, bg, lru_a[l][None, :])

        y_a = _attn_branch(z3, attn_lq1[l][None, :], attn_lk1[l][None, :],
                           attn_lq2[l][None, :], attn_lk2[l][None, :],
                           subln_g[l][None, :], lam_init)

        h2 = _merge(h2, y_r.reshape(n_b * seq, d), y_a.reshape(n_b * seq, d), z2,
                    w_br_rnn[l].astype(BF16), w_br_attn[l].astype(BF16),
                    w_out[l].astype(BF16), post_g[l][None, :])
    return h2.reshape(n_b, seq, d)
```

```python
import functools
import math

import jax
import jax.numpy as jnp
import numpy as np
from jax import lax
from jax.experimental import pallas as pl
from jax.experimental.pallas import tpu as pltpu

F32 = jnp.float32
BF16 = jnp.bfloat16

D_MODEL = 1024
RNN_BLOCKS = 16
RNN_BLOCK = 64
CONV_WIDTH = 4
LRU_C = 8.0
N_HEADS = 8
HEAD_DIM = 64
V_DIM = 128
NORM_EPS = 1e-6
D_IN_TOTAL = 8192

LANES = 128
COL_XR, COL_ZR, COL_Q, COL_K, COL_V, COL_ZA, COL_GM = 0, 8, 16, 24, 32, 40, 48

VMEM_LIMIT = 56 * 1024 * 1024


def _sigmoid(x):
    return 0.5 * jnp.tanh(0.5 * x) + 0.5


def _inproj_kernel(x_ref, g_ref, w_ref, o_ref, u_ref):
    @pl.when(pl.program_id(1) == 0)
    def _():
        x = x_ref[...]
        ms = jnp.mean(x * x, axis=-1, keepdims=True)
        u_ref[...] = (x * lax.rsqrt(ms + NORM_EPS) * g_ref[...]).astype(BF16)

    o_ref[...] = jnp.dot(u_ref[...], w_ref[...],
                         preferred_element_type=F32).astype(BF16)


def _in_proj(x2, g, w, tm=1024, tn=1024):
    n_tok = x2.shape[0]
    return pl.pallas_call(
        _inproj_kernel,
        grid=(n_tok // tm, D_IN_TOTAL // tn),
        in_specs=[
            pl.BlockSpec((tm, D_MODEL), lambda i, j: (i, 0)),
            pl.BlockSpec((1, D_MODEL), lambda i, j: (0, 0)),
            pl.BlockSpec((D_MODEL, tn), lambda i, j: (0, j)),
        ],
        out_specs=pl.BlockSpec((tm, tn), lambda i, j: (i, j)),
        out_shape=jax.ShapeDtypeStruct((n_tok, D_IN_TOTAL), BF16),
        scratch_shapes=[pltpu.VMEM((tm, D_MODEL), BF16)],
        compiler_params=pltpu.CompilerParams(
            dimension_semantics=("arbitrary", "arbitrary"),
            vmem_limit_bytes=VMEM_LIMIT),
        name="in_proj",
    )(x2, g, w)


RNN_CB = 256
RNN_T = 256
RNN_SLABS = RNN_CB // LANES
PREV_ROWS = 8


def _rnn_kernel(xr_ref, zr_ref, cw_ref, cb_ref, wg_ref, bg_ref, la_ref, o_ref,
                xe_ref, a_ref, b_ref, h_ref):
    n_b = xr_ref.shape[0]
    t_blk = xr_ref.shape[1]

    @pl.when(pl.program_id(1) == 0)
    def _():
        xe_ref[:, 0:PREV_ROWS, :] = jnp.zeros((n_b, PREV_ROWS, RNN_CB), F32)
        h_ref[...] = jnp.zeros_like(h_ref)

    cw = cw_ref[...]
    cbias = cb_ref[...]
    bg = bg_ref[0]
    la = la_ref[...]
    log_a_scale = -LRU_C * (jnp.maximum(-la, 0.0)
                            + jnp.log1p(jnp.exp(-jnp.abs(la))))

    for b in range(n_b):
        xe_ref[b, PREV_ROWS:PREV_ROWS + t_blk, :] = xr_ref[b].astype(F32)
        xc = cbias + cw[CONV_WIDTH - 1:CONV_WIDTH, :] * xe_ref[b, PREV_ROWS:PREV_ROWS + t_blk, :]
        for tap in range(CONV_WIDTH - 1):
            off = PREV_ROWS - (CONV_WIDTH - 1) + tap
            xc = xc + cw[tap:tap + 1, :] * xe_ref[b, off:off + t_blk, :]
        xe_ref[b, 0:PREV_ROWS, :] = xe_ref[b, t_blk:t_blk + PREV_ROWS, :]

        gates = jnp.dot(xc.astype(BF16), wg_ref[0], preferred_element_type=F32) + bg
        r = _sigmoid(gates[:, :RNN_CB])
        i = _sigmoid(gates[:, RNN_CB:])
        a = jnp.exp(log_a_scale * r)
        bb = jnp.sqrt(1.0 - a * a) * (i * xc)
        for s in range(RNN_SLABS):
            a_ref[s, pl.ds(b, t_blk, stride=n_b), :] = a[:, s * LANES:(s + 1) * LANES]
            b_ref[s, pl.ds(b, t_blk, stride=n_b), :] = bb[:, s * LANES:(s + 1) * LANES]

    def scan_step(t, hs):
        row = pl.multiple_of(t * n_b, n_b)
        out = []
        for s in range(RNN_SLABS):
            h = a_ref[s, pl.ds(row, n_b), :] * hs[s] + b_ref[s, pl.ds(row, n_b), :]
            b_ref[s, pl.ds(row, n_b), :] = h
            out.append(h)
        return tuple(out)

    hs = lax.fori_loop(0, t_blk, scan_step,
                       tuple(h_ref[s] for s in range(RNN_SLABS)), unroll=8)
    for s in range(RNN_SLABS):
        h_ref[s] = hs[s]

    for b in range(n_b):
        h = jnp.concatenate(
            [b_ref[s, pl.ds(b, t_blk, stride=n_b), :] for s in range(RNN_SLABS)], axis=1)
        zr = zr_ref[b].astype(F32)
        o_ref[b] = (h * (zr * _sigmoid(zr))).astype(BF16)


def _rnn_branch(z3, conv_w, conv_b, wg, bg, lru_a):
    n_b, seq, _ = z3.shape
    n_c = D_MODEL // RNN_CB
    return pl.pallas_call(
        _rnn_kernel,
        grid=(n_c, seq // RNN_T),
        in_specs=[
            pl.BlockSpec((n_b, RNN_T, RNN_CB), lambda c, t: (0, t, c)),
            pl.BlockSpec((n_b, RNN_T, RNN_CB), lambda c, t: (0, t, n_c + c)),
            pl.BlockSpec((CONV_WIDTH, RNN_CB), lambda c, t: (0, c)),
            pl.BlockSpec((1, RNN_CB), lambda c, t: (0, c)),
            pl.BlockSpec((1, RNN_CB, 2 * RNN_CB), lambda c, t: (c, 0, 0)),
            pl.BlockSpec((1, 1, 2 * RNN_CB), lambda c, t: (c, 0, 0)),
            pl.BlockSpec((1, RNN_CB), lambda c, t: (0, c)),
        ],
        out_specs=pl.BlockSpec((n_b, RNN_T, RNN_CB), lambda c, t: (0, t, c)),
        out_shape=jax.ShapeDtypeStruct((n_b, seq, D_MODEL), BF16),
        scratch_shapes=[
            pltpu.VMEM((n_b, RNN_T + PREV_ROWS, RNN_CB), F32),
            pltpu.VMEM((RNN_SLABS, RNN_T * n_b, LANES), F32),
            pltpu.VMEM((RNN_SLABS, RNN_T * n_b, LANES), F32),
            pltpu.VMEM((RNN_SLABS, n_b, LANES), F32),
        ],
        compiler_params=pltpu.CompilerParams(
            dimension_semantics=("arbitrary", "arbitrary"),
            vmem_limit_bytes=VMEM_LIMIT),
        name="rnn",
    )(z3, z3, conv_w, conv_b, wg, bg, lru_a)


ATT_TQ = 512
ATT_KC = 256
ATT_NC = 2
ATT_VROWS = V_DIM + 16
ATT_GUARD = 64.0


def _attn_schedule(seq):
    tiles, pairs, firsts = [], [], []
    for t in range(seq // ATT_TQ):
        n_pairs = -(-((t + 1) * ATT_TQ // ATT_KC) // ATT_NC)
        for j, p in enumerate([n_pairs - 1] + list(range(n_pairs - 1))):
            tiles.append(t)
            pairs.append(p)
            firsts.append(1 if j == 0 else 0)
    return np.asarray(tiles + pairs + firsts, np.int32), len(tiles)


def _attn_kernel(sched_ref, lq1_ref, lk1_ref, lq2_ref, lk2_ref, sg_ref, q_ref, k_ref,
                 v_ref, za_ref, o_ref, q2_ref, vt_ref, qn_ref, kmax_ref, bias_ref, s_ref,
                 p0_ref, p1_ref, alpha0_ref, alpha1_ref, m_ref, aref_ref, acc_ref, flag_ref,
                 *, lam_init, n_steps):
    tq, kc, nc = ATT_TQ, ATT_KC, ATT_NC
    assert tq == nc * kc
    n_tiles = q_ref.shape[1] // tq
    n_chunks = v_ref.shape[1] // kc
    p_refs, alpha_refs = (p0_ref, p1_ref), (alpha0_ref, alpha1_ref)

    lam = (jnp.exp(jnp.sum(lq1_ref[...] * lk1_ref[...], axis=-1, keepdims=True))
           - jnp.exp(jnp.sum(lq2_ref[...] * lk2_ref[...], axis=-1, keepdims=True))
           + lam_init)

    @pl.when((pl.program_id(0) == 0) & (pl.program_id(1) == 0))
    def _():
        row = lax.broadcasted_iota(jnp.int32, (kc, 2 * tq), 0)
        col = lax.broadcasted_iota(jnp.int32, (kc, 2 * tq), 1)
        diff = row - jnp.where(col >= tq, col - tq, col)
        for u in range(nc):
            bias_ref[u] = jnp.where(diff <= -u * kc, 0.0, -jnp.inf)

    dim = lax.broadcasted_iota(jnp.int32, (2 * HEAD_DIM, tq), 0)

    def prep_q(t, _):
        start = pl.multiple_of(t * tq, tq)
        qt = q_ref[0, pl.ds(start, tq), :].T
        zero = jnp.zeros_like(qt)
        for half, keep in enumerate((dim < HEAD_DIM, dim >= HEAD_DIM)):
            qh = jnp.where(keep, qt, zero)
            q2_ref[t, :, half * tq:(half + 1) * tq] = qh
            qf = qh.astype(F32)
            qn_ref[t, :, half * tq:(half + 1) * tq] = jnp.sqrt(
                jnp.sum(qf * qf, axis=0, keepdims=True))
        return 0
    lax.fori_loop(0, n_tiles, prep_q, 0, unroll=True)

    ones_rows = jnp.where(
        lax.broadcasted_iota(jnp.int32, (ATT_VROWS - V_DIM, kc), 0) == 0, 1.0, 0.0
    ).astype(BF16)

    def prep_v(c, _):
        start = pl.multiple_of(c * kc, kc)
        vt_ref[c, 0:V_DIM, :] = v_ref[0, pl.ds(start, kc), :].T
        vt_ref[c, V_DIM:ATT_VROWS, :] = ones_rows
        return 0
    lax.fori_loop(0, n_chunks, prep_v, 0, unroll=True)

    sel_dim = lax.broadcasted_iota(jnp.int32, (2 * HEAD_DIM, LANES), 0)
    sel_col = lax.broadcasted_iota(jnp.int32, (2 * HEAD_DIM, LANES), 1)
    sel = jnp.where(sel_col == jnp.where(sel_dim < HEAD_DIM, 0, 1), 1.0, 0.0).astype(BF16)
    kcol = lax.broadcasted_iota(jnp.int32, (1, 2 * tq), 1)

    def prep_k(p, _):
        start = pl.multiple_of(p * nc * kc, nc * kc)
        kf = k_ref[0, pl.ds(start, nc * kc), :].astype(F32)
        sq = (kf * kf * (1.0 + 2.0 ** -7)).astype(BF16)
        norms = jnp.max(jnp.dot(sq, sel, preferred_element_type=F32),
                        axis=0, keepdims=True)
        kmax_ref[p] = jnp.sqrt(jnp.where(kcol < tq, norms[:, 0:1], norms[:, 1:2]))
        return 0
    lax.fori_loop(0, n_chunks // nc, prep_k, 0, unroll=True)

    acc_ref[...] = jnp.zeros_like(acc_ref)

    def tile_of(i):
        return sched_ref[i]

    def pair_of(i):
        return sched_ref[n_steps + i]

    def set_guard(i_next, ref_now):
        bound = qn_ref[tile_of(i_next)] * kmax_ref[pair_of(i_next)]
        flag_ref[0] = (jnp.max(bound - ref_now) > ATT_GUARD).astype(jnp.int32)

    def scores(i, u):
        start = pl.multiple_of((pair_of(i) * nc + u) * kc, kc)
        return jnp.dot(k_ref[0, pl.ds(start, kc), :], q2_ref[tile_of(i)],
                       preferred_element_type=F32)

    def one_pass_step(i, slot):
        ref = m_ref[...]
        set_guard(jnp.minimum(i + 1, n_steps - 1), ref)
        alpha_refs[slot][...] = jnp.exp2(aref_ref[...] - ref)
        aref_ref[...] = ref
        m_new = ref
        for u in range(nc):
            s = scores(i, u)
            m_new = jnp.maximum(m_new, jnp.max(s, axis=0, keepdims=True))
            p_refs[slot][u] = jnp.exp2(s - ref).astype(BF16)
        m_ref[...] = m_new

    def two_pass_step(i, slot, first):
        for u in range(nc):
            s = scores(i, u)
            s_ref[u] = s + bias_ref[u] if first else s
        m_blk = functools.reduce(
            jnp.maximum, [jnp.max(s_ref[u], axis=0, keepdims=True) for u in range(nc)])
        if first:
            m_new = m_blk
            alpha_refs[slot][...] = jnp.zeros_like(m_blk)
        else:
            m_new = jnp.maximum(m_ref[...], m_blk)
            alpha_refs[slot][...] = jnp.exp2(aref_ref[...] - m_new)
        set_guard(jnp.minimum(i + 1, n_steps - 1), m_new)
        aref_ref[...] = m_new
        m_ref[...] = m_new
        for u in range(nc):
            p_refs[slot][u] = jnp.exp2(s_ref[u] - m_new).astype(BF16)

    def pv_stage(i, slot):
        p = pair_of(i)
        pv = jnp.dot(vt_ref[p * nc], p_refs[slot][0], preferred_element_type=F32)
        for u in range(1, nc):
            pv = pv + jnp.dot(vt_ref[p * nc + u], p_refs[slot][u],
                              preferred_element_type=F32)
        acc = alpha_refs[slot][...] * acc_ref[...] + pv
        acc_ref[...] = acc
        return acc

    def finalize(t, acc):
        o2 = acc[0:V_DIM] * (1.0 / acc[V_DIM:V_DIM + 1])
        o = (o2[:, :tq] - lam * o2[:, tq:]).T
        o = o * lax.rsqrt(jnp.mean(o * o, axis=-1, keepdims=True) + NORM_EPS)
        o = o * sg_ref[...] * (1.0 - lam_init)
        start = pl.multiple_of(t * tq, tq)
        za = za_ref[0, pl.ds(start, tq), :].astype(F32)
        o_ref[0, pl.ds(start, tq), :] = (o * (za * _sigmoid(za))).astype(BF16)

    def iteration_of(i, slot):
        is_first = sched_ref[2 * n_steps + i]
        guarded = flag_ref[0]

        @pl.when(is_first == 1)
        def _():
            acc = pv_stage(i - 1, 1 - slot)
            finalize(tile_of(i - 1), acc)
            two_pass_step(i, slot, True)

        @pl.when((is_first == 0) & (guarded == 1))
        def _():
            pv_stage(i - 1, 1 - slot)
            two_pass_step(i, slot, False)

        @pl.when((is_first == 0) & (guarded == 0))
        def _():
            pv_stage(i - 1, 1 - slot)
            one_pass_step(i, slot)

    two_pass_step(0, 0, True)

    def body(j, _):
        iteration_of(2 * j + 1, 1)
        iteration_of(2 * j + 2, 0)
        return 0
    lax.fori_loop(0, (n_steps - 1) // 2, body, 0)
    if (n_steps - 1) % 2:
        iteration_of(n_steps - 1, (n_steps - 1) % 2)

    acc = pv_stage(n_steps - 1, (n_steps - 1) % 2)
    finalize(n_tiles - 1, acc)


def _attn_branch(z3, lq1, lk1, lq2, lk2, subln_g, lam_init):
    n_b, seq, _ = z3.shape
    sched, n_steps = _attn_schedule(seq)
    small = lambda n: pl.BlockSpec((1, n), lambda b, h: (0, 0))
    head = lambda col: pl.BlockSpec((1, seq, LANES), lambda b, h: (b, 0, col + h))
    return pl.pallas_call(
        functools.partial(_attn_kernel, lam_init=lam_init, n_steps=n_steps),
        grid=(n_b, N_HEADS),
        in_specs=[
            pl.BlockSpec(memory_space=pltpu.SMEM),
            small(HEAD_DIM), small(HEAD_DIM), small(HEAD_DIM), small(HEAD_DIM),
            small(V_DIM),
            head(COL_Q), head(COL_K), head(COL_V), head(COL_ZA),
        ],
        out_specs=pl.BlockSpec((1, seq, LANES), lambda b, h: (b, 0, h)),
        out_shape=jax.ShapeDtypeStruct((n_b, seq, N_HEADS * V_DIM), BF16),
        scratch_shapes=[
            pltpu.VMEM((seq // ATT_TQ, LANES, 2 * ATT_TQ), BF16),
            pltpu.VMEM((seq // ATT_KC, ATT_VROWS, ATT_KC), BF16),
            pltpu.VMEM((seq // ATT_TQ, 1, 2 * ATT_TQ), F32),
            pltpu.VMEM((seq // (ATT_NC * ATT_KC), 1, 2 * ATT_TQ), F32),
            pltpu.VMEM((ATT_NC, ATT_KC, 2 * ATT_TQ), F32),
            pltpu.VMEM((ATT_NC, ATT_KC, 2 * ATT_TQ), F32),
            pltpu.VMEM((ATT_NC, ATT_KC, 2 * ATT_TQ), BF16),
            pltpu.VMEM((ATT_NC, ATT_KC, 2 * ATT_TQ), BF16),
            pltpu.VMEM((1, 2 * ATT_TQ), F32),
            pltpu.VMEM((1, 2 * ATT_TQ), F32),
            pltpu.VMEM((1, 2 * ATT_TQ), F32),
            pltpu.VMEM((1, 2 * ATT_TQ), F32),
            pltpu.VMEM((ATT_VROWS, 2 * ATT_TQ), F32),
            pltpu.SMEM((1,), jnp.int32),
        ],
        compiler_params=pltpu.CompilerParams(
            dimension_semantics=("arbitrary", "arbitrary"),
            vmem_limit_bytes=VMEM_LIMIT),
        name="diff_attn",
    )(jnp.asarray(sched), lq1, lk1, lq2, lk2, subln_g, z3, z3, z3, z3)


def _merge_kernel(x_ref, yr_ref, ya_ref, gm_ref, wr_ref, wa_ref, wo_ref, pg_ref, o_ref):
    pr = jnp.dot(yr_ref[...], wr_ref[...], preferred_element_type=F32)
    pa = jnp.dot(ya_ref[...], wa_ref[...], preferred_element_type=F32)
    g = _sigmoid(gm_ref[...].astype(F32))
    m = g[:, :D_MODEL] * pr + g[:, D_MODEL:] * pa
    y = jnp.dot(m.astype(BF16), wo_ref[...], preferred_element_type=F32)
    y = y * lax.rsqrt(jnp.mean(y * y, axis=-1, keepdims=True) + NORM_EPS)
    o_ref[...] = x_ref[...] + y * pg_ref[...]


def _merge(x2, yr, ya, z2, wr, wa, wo, post_g, tm=512):
    n_tok = x2.shape[0]
    full = lambda r, c: pl.BlockSpec((r, c), lambda i: (0, 0))
    return pl.pallas_call(
        _merge_kernel,
        grid=(n_tok // tm,),
        in_specs=[
            pl.BlockSpec((tm, D_MODEL), lambda i: (i, 0)),
            pl.BlockSpec((tm, D_MODEL), lambda i: (i, 0)),
            pl.BlockSpec((tm, D_MODEL), lambda i: (i, 0)),
            pl.BlockSpec((tm, 2 * D_MODEL), lambda i: (i, COL_GM * LANES // (2 * D_MODEL))),
            full(D_MODEL, D_MODEL), full(D_MODEL, D_MODEL), full(D_MODEL, D_MODEL),
            full(1, D_MODEL),
        ],
        out_specs=pl.BlockSpec((tm, D_MODEL), lambda i: (i, 0)),
        out_shape=jax.ShapeDtypeStruct((n_tok, D_MODEL), F32),
        compiler_params=pltpu.CompilerParams(
            dimension_semantics=("arbitrary",),
            vmem_limit_bytes=VMEM_LIMIT),
        name="merge",
    )(x2, yr, ya, z2, wr, wa, wo, post_g)


def _q_column_scale():
    col = np.ones((1, D_IN_TOTAL), np.float32)
    col[:, COL_Q * LANES:COL_K * LANES] = HEAD_DIM ** -0.5 * math.log2(math.e)
    return jnp.asarray(col)


def _block_diag_gate_weights(wa, wx):
    per = RNN_CB // RNN_BLOCK

    def expand(w):
        w = w.reshape(D_MODEL // RNN_CB, per, RNN_BLOCK, RNN_BLOCK)
        eye = jnp.eye(per, dtype=w.dtype)
        return jnp.einsum('cgij,gh->cgihj', w, eye).reshape(
            D_MODEL // RNN_CB, RNN_CB, RNN_CB)

    return jnp.concatenate([expand(wa), expand(wx)], axis=-1)


def kernel(x, pre_g, post_g, w_in, conv_w, conv_b, lru_wa, lru_ba, lru_wx, lru_bx, lru_a,
           attn_lq1, attn_lk1, attn_lq2, attn_lk2, subln_g, w_br_rnn, w_br_attn, w_out):
    n_b, seq, d = x.shape
    depth = pre_g.shape[0]
    h2 = x.reshape(n_b * seq, d)
    n_c = D_MODEL // RNN_CB
    for l in range(depth):
        lam_init = 0.8 - 0.6 * math.exp(-0.3 * l)
        z2 = _in_proj(h2, pre_g[l][None, :], (w_in[l] * _q_column_scale()).astype(BF16))
        z3 = z2.reshape(n_b, seq, D_IN_TOTAL)

        wg = _block_diag_gate_weights(lru_wa[l], lru_wx[l]).astype(BF16)
        bg = jnp.concatenate([lru_ba[l].reshape(n_c, 1, RNN_CB),
                              lru_bx[l].reshape(n_c, 1, RNN_CB)], axis=-1)
        y_r = _rnn_branch(z3, conv_w[l], conv_b[l][None, :], wg, bg, lru_a[l][None, :])

        y_a = _attn_branch(z3, attn_lq1[l][None, :], attn_lk1[l][None, :],
                           attn_lq2[l][None, :], attn_lk2[l][None, :],
                           subln_g[l][None, :], lam_init)

        h2 = _merge(h2, y_r.reshape(n_b * seq, d), y_a.reshape(n_b * seq, d), z2,
                    w_br_rnn[l].astype(BF16), w_br_attn[l].astype(BF16),
                    w_out[l].astype(BF16), post_g[l][None, :])
    return h2.reshape(n_b, seq, d)
```

```python
import functools
import math

import jax
import jax.numpy as jnp
import numpy as np
from jax import lax
from jax.experimental import pallas as pl
from jax.experimental.pallas import tpu as pltpu

F32 = jnp.float32
BF16 = jnp.bfloat16

D_MODEL = 1024
RNN_BLOCKS = 16
RNN_BLOCK = 64
CONV_WIDTH = 4
LRU_C = 8.0
N_HEADS = 8
HEAD_DIM = 64
V_DIM = 128
NORM_EPS = 1e-6
D_IN_TOTAL = 8192

LANES = 128
COL_XR, COL_ZR, COL_Q, COL_K, COL_V, COL_ZA, COL_GM = 0, 8, 16, 24, 32, 40, 48

VMEM_LIMIT = 56 * 1024 * 1024


def _sigmoid(x):
    return 0.5 * jnp.tanh(0.5 * x) + 0.5


def _inproj_kernel(x_ref, g_ref, w_ref, o_ref, u_ref):
    @pl.when(pl.program_id(1) == 0)
    def _():
        x = x_ref[...]
        ms = jnp.mean(x * x, axis=-1, keepdims=True)
        u_ref[...] = (x * lax.rsqrt(ms + NORM_EPS) * g_ref[...]).astype(BF16)

    o_ref[...] = jnp.dot(u_ref[...], w_ref[...],
                         preferred_element_type=F32).astype(BF16)


def _in_proj(x2, g, w, tm=1024, tn=1024):
    n_tok = x2.shape[0]
    return pl.pallas_call(
        _inproj_kernel,
        grid=(n_tok // tm, D_IN_TOTAL // tn),
        in_specs=[
            pl.BlockSpec((tm, D_MODEL), lambda i, j: (i, 0)),
            pl.BlockSpec((1, D_MODEL), lambda i, j: (0, 0)),
            pl.BlockSpec((D_MODEL, tn), lambda i, j: (0, j)),
        ],
        out_specs=pl.BlockSpec((tm, tn), lambda i, j: (i, j)),
        out_shape=jax.ShapeDtypeStruct((n_tok, D_IN_TOTAL), BF16),
        scratch_shapes=[pltpu.VMEM((tm, D_MODEL), BF16)],
        compiler_params=pltpu.CompilerParams(
            dimension_semantics=("arbitrary", "arbitrary"),
            vmem_limit_bytes=VMEM_LIMIT),
        name="in_proj",
    )(x2, g, w)


RNN_CB = 256
RNN_T = 256
RNN_SLABS = RNN_CB // LANES
PREV_ROWS = 8


def _rnn_kernel(xr_ref, zr_ref, cw_ref, cb_ref, wg_ref, bg_ref, la_ref, o_ref,
                xe_ref, a_ref, b_ref, h_ref):
    n_b = xr_ref.shape[0]
    t_blk = xr_ref.shape[1]

    @pl.when(pl.program_id(1) == 0)
    def _():
        xe_ref[:, 0:PREV_ROWS, :] = jnp.zeros((n_b, PREV_ROWS, RNN_CB), F32)
        h_ref[...] = jnp.zeros_like(h_ref)

    cw = cw_ref[...]
    cbias = cb_ref[...]
    bg = bg_ref[0]
    la = la_ref[...]
    log_a_scale = -LRU_C * (jnp.maximum(-la, 0.0)
                            + jnp.log1p(jnp.exp(-jnp.abs(la))))

    for b in range(n_b):
        xe_ref[b, PREV_ROWS:PREV_ROWS + t_blk, :] = xr_ref[b].astype(F32)
        xc = cbias + cw[CONV_WIDTH - 1:CONV_WIDTH, :] * xe_ref[b, PREV_ROWS:PREV_ROWS + t_blk, :]
        for tap in range(CONV_WIDTH - 1):
            off = PREV_ROWS - (CONV_WIDTH - 1) + tap
            xc = xc + cw[tap:tap + 1, :] * xe_ref[b, off:off + t_blk, :]
        xe_ref[b, 0:PREV_ROWS, :] = xe_ref[b, t_blk:t_blk + PREV_ROWS, :]

        gates = jnp.dot(xc.astype(BF16), wg_ref[0], preferred_element_type=F32) + bg
        r = _sigmoid(gates[:, :RNN_CB])
        i = _sigmoid(gates[:, RNN_CB:])
        a = jnp.exp(log_a_scale * r)
        bb = jnp.sqrt(1.0 - a * a) * (i * xc)
        for s in range(RNN_SLABS):
            a_ref[s, pl.ds(b, t_blk, stride=n_b), :] = a[:, s * LANES:(s + 1) * LANES]
            b_ref[s, pl.ds(b, t_blk, stride=n_b), :] = bb[:, s * LANES:(s + 1) * LANES]

    def scan_step(t, hs):
        row = pl.multiple_of(t * n_b, n_b)
        out = []
        for s in range(RNN_SLABS):
            h = a_ref[s, pl.ds(row, n_b), :] * hs[s] + b_ref[s, pl.ds(row, n_b), :]
            b_ref[s, pl.ds(row, n_b), :] = h
            out.append(h)
        return tuple(out)

    hs = lax.fori_loop(0, t_blk, scan_step,
                       tuple(h_ref[s] for s in range(RNN_SLABS)), unroll=8)
    for s in range(RNN_SLABS):
        h_ref[s] = hs[s]

    for b in range(n_b):
        h = jnp.concatenate(
            [b_ref[s, pl.ds(b, t_blk, stride=n_b), :] for s in range(RNN_SLABS)], axis=1)
        zr = zr_ref[b].astype(F32)
        o_ref[b] = (h * (zr * _sigmoid(zr))).astype(BF16)


def _rnn_branch(z3, conv_w, conv_b, wg, bg, lru_a):
    n_b, seq, _ = z3.shape
    n_c = D_MODEL // RNN_CB
    return pl.pallas_call(
        _rnn_kernel,
        grid=(n_c, seq // RNN_T),
        in_specs=[
            pl.BlockSpec((n_b, RNN_T, RNN_CB), lambda c, t: (0, t, c)),
            pl.BlockSpec((n_b, RNN_T, RNN_CB), lambda c, t: (0, t, n_c + c)),
            pl.BlockSpec((CONV_WIDTH, RNN_CB), lambda c, t: (0, c)),
            pl.BlockSpec((1, RNN_CB), lambda c, t: (0, c)),
            pl.BlockSpec((1, RNN_CB, 2 * RNN_CB), lambda c, t: (c, 0, 0)),
            pl.BlockSpec((1, 1, 2 * RNN_CB), lambda c, t: (c, 0, 0)),
            pl.BlockSpec((1, RNN_CB), lambda c, t: (0, c)),
        ],
        out_specs=pl.BlockSpec((n_b, RNN_T, RNN_CB), lambda c, t: (0, t, c)),
        out_shape=jax.ShapeDtypeStruct((n_b, seq, D_MODEL), BF16),
        scratch_shapes=[
            pltpu.VMEM((n_b, RNN_T + PREV_ROWS, RNN_CB), F32),
            pltpu.VMEM((RNN_SLABS, RNN_T * n_b, LANES), F32),
            pltpu.VMEM((RNN_SLABS, RNN_T * n_b, LANES), F32),
            pltpu.VMEM((RNN_SLABS, n_b, LANES), F32),
        ],
        compiler_params=pltpu.CompilerParams(
            dimension_semantics=("arbitrary", "arbitrary"),
            vmem_limit_bytes=VMEM_LIMIT),
        name="rnn",
    )(z3, z3, conv_w, conv_b, wg, bg, lru_a)


ATT_TQ = 512
ATT_KC = 512
ATT_NC = 1
ATT_VROWS = V_DIM + 16
ATT_BOUND = 32.0


def _attn_kernel(lq1_ref, lk1_ref, lq2_ref, lk2_ref, sg_ref, q_ref, k_ref, v_ref, za_ref,
                 o_ref, q2_ref, vt_ref, bias_ref, s_ref, m_ref, acc_ref, *, lam_init):
    tq, kc, nc = ATT_TQ, ATT_KC, ATT_NC
    assert tq == nc * kc
    n_tiles = q_ref.shape[1] // tq
    n_chunks = v_ref.shape[1] // kc

    lam = (jnp.exp(jnp.sum(lq1_ref[...] * lk1_ref[...], axis=-1, keepdims=True))
           - jnp.exp(jnp.sum(lq2_ref[...] * lk2_ref[...], axis=-1, keepdims=True))
           + lam_init)

    @pl.when((pl.program_id(0) == 0) & (pl.program_id(1) == 0))
    def _():
        row = lax.broadcasted_iota(jnp.int32, (kc, 2 * tq), 0)
        col = lax.broadcasted_iota(jnp.int32, (kc, 2 * tq), 1)
        diff = row - jnp.where(col >= tq, col - tq, col)
        for u in range(nc):
            bias_ref[u] = jnp.where(diff <= -u * kc, 0.0, -jnp.inf)

    dim = lax.broadcasted_iota(jnp.int32, (2 * HEAD_DIM, tq), 0)
    qn2 = None
    for t in range(n_tiles):
        qt = q_ref[0, t * tq:(t + 1) * tq, :].T
        zero = jnp.zeros_like(qt)
        norms = []
        for half, keep in enumerate((dim < HEAD_DIM, dim >= HEAD_DIM)):
            qh = jnp.where(keep, qt, zero)
            q2_ref[t, :, half * tq:(half + 1) * tq] = qh
            qf = qh.astype(F32)
            norms.append(jnp.sum(qf * qf, axis=0, keepdims=True))
        n2 = jnp.concatenate(norms, axis=1)
        qn2 = n2 if qn2 is None else jnp.maximum(qn2, n2)

    ones_rows = jnp.where(
        lax.broadcasted_iota(jnp.int32, (ATT_VROWS - V_DIM, kc), 0) == 0, 1.0, 0.0
    ).astype(BF16)
    for c in range(n_chunks):
        vt_ref[c, 0:V_DIM, :] = v_ref[0, c * kc:(c + 1) * kc, :].T
        vt_ref[c, V_DIM:ATT_VROWS, :] = ones_rows

    sel_dim = lax.broadcasted_iota(jnp.int32, (2 * HEAD_DIM, LANES), 0)
    sel_col = lax.broadcasted_iota(jnp.int32, (2 * HEAD_DIM, LANES), 1)
    sel = jnp.where(sel_col == jnp.where(sel_dim < HEAD_DIM, 0, 1), 1.0, 0.0).astype(BF16)
    kn2 = None
    for c in range(n_chunks):
        kf = k_ref[0, c * kc:(c + 1) * kc, :].astype(F32)
        sq = (kf * kf * (1.0 + 2.0 ** -7)).astype(BF16)
        n2 = jnp.max(jnp.dot(sq, sel, preferred_element_type=F32), axis=0, keepdims=True)
        kn2 = n2 if kn2 is None else jnp.maximum(kn2, n2)
    kcol = lax.broadcasted_iota(jnp.int32, (1, 2 * tq), 1)
    bound2 = qn2 * jnp.where(kcol < tq, kn2[:, 0:1], kn2[:, 1:2])
    bounded = jnp.max(bound2) <= ATT_BOUND * ATT_BOUND

    def scores(t, c):
        if isinstance(c, int):
            kt = k_ref[0, c * kc:(c + 1) * kc, :]
        else:
            kt = k_ref[0, pl.ds(pl.multiple_of(c * kc, kc), kc), :]
        return jnp.dot(kt, q2_ref[t], preferred_element_type=F32)

    def finalize(t, acc):
        o2 = acc[0:V_DIM] * (1.0 / acc[V_DIM:V_DIM + 1])
        o = (o2[:, :tq] - lam * o2[:, tq:]).T
        o = o * lax.rsqrt(jnp.mean(o * o, axis=-1, keepdims=True) + NORM_EPS)
        o = o * sg_ref[...] * (1.0 - lam_init)
        start = t * tq if isinstance(t, int) else pl.multiple_of(t * tq, tq)
        za = za_ref[0, pl.ds(start, tq), :].astype(F32)
        o_ref[0, pl.ds(start, tq), :] = (o * (za * _sigmoid(za))).astype(BF16)

    @pl.when(bounded)
    def _():
        for t in range(n_tiles):
            acc = None
            for c in range((t + 1) * nc):
                s = scores(t, c)
                if c >= t * nc:
                    s = s + bias_ref[c - t * nc]
                pv = jnp.dot(vt_ref[c], jnp.exp2(s).astype(BF16),
                             preferred_element_type=F32)
                acc = pv if acc is None else acc + pv
            finalize(t, acc)

    @pl.when(jnp.logical_not(bounded))
    def _():
        def step(t, pair, first):
            for u in range(nc):
                s = scores(t, pair * nc + u)
                s_ref[u] = s + bias_ref[u] if first else s
            m_blk = functools.reduce(
                jnp.maximum, [jnp.max(s_ref[u], axis=0, keepdims=True) for u in range(nc)])
            if first:
                m_new = m_blk
            else:
                m_new = jnp.maximum(m_ref[...], m_blk)
                alpha = jnp.exp2(m_ref[...] - m_new)
            m_ref[...] = m_new
            pv = None
            for u in range(nc):
                d = jnp.dot(vt_ref[pair * nc + u], jnp.exp2(s_ref[u] - m_new).astype(BF16),
                            preferred_element_type=F32)
                pv = d if pv is None else pv + d
            acc_ref[...] = pv if first else alpha * acc_ref[...] + pv

        def tile_body(t, _):
            step(t, t, True)

            def rest(pair, _):
                step(t, pair, False)
                return 0
            lax.fori_loop(0, t, rest, 0)
            finalize(t, acc_ref[...])
            return 0
        lax.fori_loop(0, n_tiles, tile_body, 0)


def _attn_branch(z3, lq1, lk1, lq2, lk2, subln_g, lam_init):
    n_b, seq, _ = z3.shape
    small = lambda n: pl.BlockSpec((1, n), lambda b, h: (0, 0))
    head = lambda col: pl.BlockSpec((1, seq, LANES), lambda b, h: (b, 0, col + h))
    return pl.pallas_call(
        functools.partial(_attn_kernel, lam_init=lam_init),
        grid=(n_b, N_HEADS),
        in_specs=[
            small(HEAD_DIM), small(HEAD_DIM), small(HEAD_DIM), small(HEAD_DIM),
            small(V_DIM),
            head(COL_Q), head(COL_K), head(COL_V), head(COL_ZA),
        ],
        out_specs=pl.BlockSpec((1, seq, LANES), lambda b, h: (b, 0, h)),
        out_shape=jax.ShapeDtypeStruct((n_b, seq, N_HEADS * V_DIM), BF16),
        scratch_shapes=[
            pltpu.VMEM((seq // ATT_TQ, LANES, 2 * ATT_TQ), BF16),
            pltpu.VMEM((seq // ATT_KC, ATT_VROWS, ATT_KC), BF16),
            pltpu.VMEM((ATT_NC, ATT_KC, 2 * ATT_TQ), F32),
            pltpu.VMEM((ATT_NC, ATT_KC, 2 * ATT_TQ), F32),
            pltpu.VMEM((1, 2 * ATT_TQ), F32),
            pltpu.VMEM((ATT_VROWS, 2 * ATT_TQ), F32),
        ],
        compiler_params=pltpu.CompilerParams(
            dimension_semantics=("arbitrary", "arbitrary"),
            vmem_limit_bytes=VMEM_LIMIT),
        name="diff_attn",
    )(lq1, lk1, lq2, lk2, subln_g, z3, z3, z3, z3)


def _merge_kernel(x_ref, yr_ref, ya_ref, gm_ref, wr_ref, wa_ref, wo_ref, pg_ref, o_ref):
    pr = jnp.dot(yr_ref[...], wr_ref[...], preferred_element_type=F32)
    pa = jnp.dot(ya_ref[...], wa_ref[...], preferred_element_type=F32)
    g = _sigmoid(gm_ref[...].astype(F32))
    m = g[:, :D_MODEL] * pr + g[:, D_MODEL:] * pa
    y = jnp.dot(m.astype(BF16), wo_ref[...], preferred_element_type=F32)
    y = y * lax.rsqrt(jnp.mean(y * y, axis=-1, keepdims=True) + NORM_EPS)
    o_ref[...] = x_ref[...] + y * pg_ref[...]


def _merge(x2, yr, ya, z2, wr, wa, wo, post_g, tm=512):
    n_tok = x2.shape[0]
    full = lambda r, c: pl.BlockSpec((r, c), lambda i: (0, 0))
    return pl.pallas_call(
        _merge_kernel,
        grid=(n_tok // tm,),
        in_specs=[
            pl.BlockSpec((tm, D_MODEL), lambda i: (i, 0)),
            pl.BlockSpec((tm, D_MODEL), lambda i: (i, 0)),
            pl.BlockSpec((tm, D_MODEL), lambda i: (i, 0)),
            pl.BlockSpec((tm, 2 * D_MODEL), lambda i: (i, COL_GM * LANES // (2 * D_MODEL))),
            full(D_MODEL, D_MODEL), full(D_MODEL, D_MODEL), full(D_MODEL, D_MODEL),
            full(1, D_MODEL),
        ],
        out_specs=pl.BlockSpec((tm, D_MODEL), lambda i: (i, 0)),
        out_shape=jax.ShapeDtypeStruct((n_tok, D_MODEL), F32),
        compiler_params=pltpu.CompilerParams(
            dimension_semantics=("arbitrary",),
            vmem_limit_bytes=VMEM_LIMIT),
        name="merge",
    )(x2, yr, ya, z2, wr, wa, wo, post_g)


def _q_column_scale():
    col = np.ones((1, D_IN_TOTAL), np.float32)
    col[:, COL_Q * LANES:COL_K * LANES] = HEAD_DIM ** -0.5 * math.log2(math.e)
    return jnp.asarray(col)


def _block_diag_gate_weights(wa, wx):
    per = RNN_CB // RNN_BLOCK

    def expand(w):
        w = w.reshape(D_MODEL // RNN_CB, per, RNN_BLOCK, RNN_BLOCK)
        eye = jnp.eye(per, dtype=w.dtype)
        return jnp.einsum('cgij,gh->cgihj', w, eye).reshape(
            D_MODEL // RNN_CB, RNN_CB, RNN_CB)

    return jnp.concatenate([expand(wa), expand(wx)], axis=-1)


def kernel(x, pre_g, post_g, w_in, conv_w, conv_b, lru_wa, lru_ba, lru_wx, lru_bx, lru_a,
           attn_lq1, attn_lk1, attn_lq2, attn_lk2, subln_g, w_br_rnn, w_br_attn, w_out):
    n_b, seq, d = x.shape
    depth = pre_g.shape[0]
    h2 = x.reshape(n_b * seq, d)
    n_c = D_MODEL // RNN_CB
    for l in range(depth):
        lam_init = 0.8 - 0.6 * math.exp(-0.3 * l)
        z2 = _in_proj(h2, pre_g[l][None, :], (w_in[l] * _q_column_scale()).astype(BF16))
        z3 = z2.reshape(n_b, seq, D_IN_TOTAL)

        wg = _block_diag_gate_weights(lru_wa[l], lru_wx[l]).astype(BF16)
        bg = jnp.concatenate([lru_ba[l].reshape(n_c, 1, RNN_CB),
                              lru_bx[l].reshape(n_c, 1, RNN_CB)], axis=-1)
        y_r = _rnn_branch(z3, conv_w[l], conv_b[l][None, :], wg, bg, lru_a[l][None, :])

        y_a = _attn_branch(z3, attn_lq1[l][None, :], attn_lk1[l][None, :],
                           attn_lq2[l][None, :], attn_lk2[l][None, :],
                           subln_g[l][None, :], lam_init)

        h2 = _merge(h2, y_r.reshape(n_b * seq, d), y_a.reshape(n_b * seq, d), z2,
                    w_br_rnn[l].astype(BF16), w_br_attn[l].astype(BF16),
                    w_out[l].astype(BF16), post_g[l][None, :])
    return h2.reshape(n_b, seq, d)
```

```python
import functools
import math

import jax
import jax.numpy as jnp
import numpy as np
from jax import lax
from jax.experimental import pallas as pl
from jax.experimental.pallas import tpu as pltpu

F32 = jnp.float32
BF16 = jnp.bfloat16

D_MODEL = 1024
RNN_BLOCKS = 16
RNN_BLOCK = 64
CONV_WIDTH = 4
LRU_C = 8.0
N_HEADS = 8
HEAD_DIM = 64
V_DIM = 128
NORM_EPS = 1e-6
D_IN_TOTAL = 8192

LANES = 128
COL_XR, COL_ZR, COL_Q, COL_K, COL_V, COL_ZA, COL_GM = 0, 8, 16, 24, 32, 40, 48

VMEM_LIMIT = 56 * 1024 * 1024


LOG2E = math.log2(math.e)


def _sigmoid(x):
    return 1.0 / (1.0 + jnp.exp2(x * (-LOG2E)))


INPROJ_TM = 512
INPROJ_TN = 1024


def _inproj_kernel(x_ref, g_ref, w_ref, o_ref):
    x = x_ref[...]
    ms = jnp.mean(x * x, axis=-1, keepdims=True)
    u = (x * lax.rsqrt(ms + NORM_EPS) * g_ref[...]).astype(BF16)
    for j in range(D_IN_TOTAL // INPROJ_TN):
        cols = slice(j * INPROJ_TN, (j + 1) * INPROJ_TN)
        o_ref[:, cols] = jnp.dot(u, w_ref[:, cols], preferred_element_type=F32).astype(BF16)


def _in_proj(x2, g, w):
    n_tok = x2.shape[0]
    return pl.pallas_call(
        _inproj_kernel,
        grid=(n_tok // INPROJ_TM,),
        in_specs=[
            pl.BlockSpec((INPROJ_TM, D_MODEL), lambda i: (i, 0)),
            pl.BlockSpec((1, D_MODEL), lambda i: (0, 0)),
            pl.BlockSpec((D_MODEL, D_IN_TOTAL), lambda i: (0, 0),
                         pipeline_mode=pl.Buffered(1)),
        ],
        out_specs=pl.BlockSpec((INPROJ_TM, D_IN_TOTAL), lambda i: (i, 0)),
        out_shape=jax.ShapeDtypeStruct((n_tok, D_IN_TOTAL), BF16),
        compiler_params=pltpu.CompilerParams(
            dimension_semantics=("arbitrary",),
            vmem_limit_bytes=VMEM_LIMIT),
        name="in_proj",
    )(x2, g, w)


RNN_CB = 512
RNN_T = 256
RNN_SLABS = RNN_CB // LANES
RNN_RC = 512
SQRT_FLOOR = 1e-30


def _rnn_kernel(xr_ref, zr_ref, cw_ref, cb_ref, wg_ref, bg_ref, la_ref, o_ref,
                xt_ref, a_ref, b_ref, h_ref):
    n_b = xr_ref.shape[0]
    t_blk = xr_ref.shape[1]
    rows = t_blk * n_b
    hdr = (CONV_WIDTH - 1) * n_b

    @pl.when(pl.program_id(1) == 0)
    def _():
        xt_ref[:, 0:hdr, :] = jnp.zeros((RNN_SLABS, hdr, LANES), F32)
        h_ref[...] = jnp.zeros_like(h_ref)

    for b in range(n_b):
        x = xr_ref[b].astype(F32)
        for s in range(RNN_SLABS):
            xt_ref[s, pl.ds(hdr + b, t_blk, stride=n_b), :] = x[:, s * LANES:(s + 1) * LANES]

    cw = cw_ref[...]
    cbias = cb_ref[...]
    bg = bg_ref[0]
    la = la_ref[...]
    log2a_scale = (-LRU_C * LOG2E) * (jnp.maximum(-la, 0.0)
                                      + jnp.log1p(jnp.exp(-jnp.abs(la))))

    for c in range(rows // RNN_RC):
        r0 = hdr + c * RNN_RC
        xc = cbias
        for tap in range(CONV_WIDTH):
            off = r0 - (CONV_WIDTH - 1 - tap) * n_b
            x_tap = jnp.concatenate(
                [xt_ref[s, off:off + RNN_RC, :] for s in range(RNN_SLABS)], axis=1)
            xc = xc + cw[tap:tap + 1, :] * x_tap
        gates = jnp.dot(xc.astype(BF16), wg_ref[0], preferred_element_type=F32) + bg
        sg = _sigmoid(gates)
        a = jnp.exp2(log2a_scale * sg[:, :RNN_CB])
        y = 1.0 - a * a
        bb = (y * lax.rsqrt(jnp.maximum(y, SQRT_FLOOR))) * (sg[:, RNN_CB:] * xc)
        for s in range(RNN_SLABS):
            a_ref[s, c * RNN_RC:(c + 1) * RNN_RC, :] = a[:, s * LANES:(s + 1) * LANES]
            b_ref[s, c * RNN_RC:(c + 1) * RNN_RC, :] = bb[:, s * LANES:(s + 1) * LANES]

    for s in range(RNN_SLABS):
        xt_ref[s, 0:hdr, :] = xt_ref[s, rows:rows + hdr, :]

    def scan_step(t, hs):
        row = pl.multiple_of(t * n_b, n_b)
        out = []
        for s in range(RNN_SLABS):
            h = a_ref[s, pl.ds(row, n_b), :] * hs[s] + b_ref[s, pl.ds(row, n_b), :]
            b_ref[s, pl.ds(row, n_b), :] = h
            out.append(h)
        return tuple(out)

    hs = lax.fori_loop(0, t_blk, scan_step,
                       tuple(h_ref[s] for s in range(RNN_SLABS)), unroll=8)
    for s in range(RNN_SLABS):
        h_ref[s] = hs[s]

    for b in range(n_b):
        h = jnp.concatenate(
            [b_ref[s, pl.ds(b, t_blk, stride=n_b), :] for s in range(RNN_SLABS)], axis=1)
        zr = zr_ref[b].astype(F32)
        o_ref[b] = (h * (zr * _sigmoid(zr))).astype(BF16)


def _rnn_branch(z3, conv_w, conv_b, wg, bg, lru_a):
    n_b, seq, _ = z3.shape
    n_c = D_MODEL // RNN_CB
    rows = RNN_T * n_b
    return pl.pallas_call(
        _rnn_kernel,
        grid=(n_c, seq // RNN_T),
        in_specs=[
            pl.BlockSpec((n_b, RNN_T, RNN_CB), lambda c, t: (0, t, c)),
            pl.BlockSpec((n_b, RNN_T, RNN_CB), lambda c, t: (0, t, n_c + c)),
            pl.BlockSpec((CONV_WIDTH, RNN_CB), lambda c, t: (0, c)),
            pl.BlockSpec((1, RNN_CB), lambda c, t: (0, c)),
            pl.BlockSpec((1, RNN_CB, 2 * RNN_CB), lambda c, t: (c, 0, 0)),
            pl.BlockSpec((1, 1, 2 * RNN_CB), lambda c, t: (c, 0, 0)),
            pl.BlockSpec((1, RNN_CB), lambda c, t: (0, c)),
        ],
        out_specs=pl.BlockSpec((n_b, RNN_T, RNN_CB), lambda c, t: (0, t, c)),
        out_shape=jax.ShapeDtypeStruct((n_b, seq, D_MODEL), BF16),
        scratch_shapes=[
            pltpu.VMEM((RNN_SLABS, (CONV_WIDTH - 1) * n_b + rows, LANES), F32),
            pltpu.VMEM((RNN_SLABS, rows, LANES), F32),
            pltpu.VMEM((RNN_SLABS, rows, LANES), F32),
            pltpu.VMEM((RNN_SLABS, n_b, LANES), F32),
        ],
        compiler_params=pltpu.CompilerParams(
            dimension_semantics=("arbitrary", "arbitrary"),
            vmem_limit_bytes=VMEM_LIMIT),
        name="rnn",
    )(z3, z3, conv_w, conv_b, wg, bg, lru_a)


ATT_TQ = 512
ATT_KC = 512
ATT_NC = 1
ATT_VROWS = V_DIM + 16
ATT_BOUND = 32.0


def _attn_kernel(lq1_ref, lk1_ref, lq2_ref, lk2_ref, sg_ref, q_ref, k_ref, v_ref, za_ref,
                 o_ref, q2_ref, vt_ref, bias_ref, s_ref, m_ref, acc_ref, *, lam_init):
    tq, kc, nc = ATT_TQ, ATT_KC, ATT_NC
    assert tq == nc * kc
    n_tiles = q_ref.shape[1] // tq
    n_chunks = v_ref.shape[1] // kc

    lam = (jnp.exp(jnp.sum(lq1_ref[...] * lk1_ref[...], axis=-1, keepdims=True))
           - jnp.exp(jnp.sum(lq2_ref[...] * lk2_ref[...], axis=-1, keepdims=True))
           + lam_init)

    @pl.when((pl.program_id(0) == 0) & (pl.program_id(1) == 0))
    def _():
        row = lax.broadcasted_iota(jnp.int32, (kc, 2 * tq), 0)
        col = lax.broadcasted_iota(jnp.int32, (kc, 2 * tq), 1)
        diff = row - jnp.where(col >= tq, col - tq, col)
        for u in range(nc):
            bias_ref[u] = jnp.where(diff <= -u * kc, 0.0, -jnp.inf)

    dim = lax.broadcasted_iota(jnp.int32, (2 * HEAD_DIM, tq), 0)
    qn2 = None
    for t in range(n_tiles):
        qt = q_ref[0, t * tq:(t + 1) * tq, :].T
        zero = jnp.zeros_like(qt)
        norms = []
        for half, keep in enumerate((dim < HEAD_DIM, dim >= HEAD_DIM)):
            qh = jnp.where(keep, qt, zero)
            q2_ref[t, :, half * tq:(half + 1) * tq] = qh
            qf = qh.astype(F32)
            norms.append(jnp.sum(qf * qf, axis=0, keepdims=True))
        n2 = jnp.concatenate(norms, axis=1)
        qn2 = n2 if qn2 is None else jnp.maximum(qn2, n2)

    ones_rows = jnp.where(
        lax.broadcasted_iota(jnp.int32, (ATT_VROWS - V_DIM, kc), 0) == 0, 1.0, 0.0
    ).astype(BF16)
    for c in range(n_chunks):
        vt_ref[c, 0:V_DIM, :] = v_ref[0, c * kc:(c + 1) * kc, :].T
        vt_ref[c, V_DIM:ATT_VROWS, :] = ones_rows

    sel_dim = lax.broadcasted_iota(jnp.int32, (2 * HEAD_DIM, LANES), 0)
    sel_col = lax.broadcasted_iota(jnp.int32, (2 * HEAD_DIM, LANES), 1)
    sel = jnp.where(sel_col == jnp.where(sel_dim < HEAD_DIM, 0, 1), 1.0, 0.0).astype(BF16)
    kn2 = None
    for c in range(n_chunks):
        kf = k_ref[0, c * kc:(c + 1) * kc, :].astype(F32)
        sq = (kf * kf * (1.0 + 2.0 ** -7)).astype(BF16)
        n2 = jnp.max(jnp.dot(sq, sel, preferred_element_type=F32), axis=0, keepdims=True)
        kn2 = n2 if kn2 is None else jnp.maximum(kn2, n2)
    kcol = lax.broadcasted_iota(jnp.int32, (1, 2 * tq), 1)
    bound2 = qn2 * jnp.where(kcol < tq, kn2[:, 0:1], kn2[:, 1:2])
    bounded = jnp.max(bound2) <= ATT_BOUND * ATT_BOUND

    def scores(t, c):
        if isinstance(c, int):
            kt = k_ref[0, c * kc:(c + 1) * kc, :]
        else:
            kt = k_ref[0, pl.ds(pl.multiple_of(c * kc, kc), kc), :]
        return jnp.dot(kt, q2_ref[t], preferred_element_type=F32)

    def finalize(t, acc):
        o2 = acc[0:V_DIM] * (1.0 / acc[V_DIM:V_DIM + 1])
        o = (o2[:, :tq] - lam * o2[:, tq:]).T
        o = o * lax.rsqrt(jnp.mean(o * o, axis=-1, keepdims=True) + NORM_EPS)
        o = o * sg_ref[...] * (1.0 - lam_init)
        start = t * tq if isinstance(t, int) else pl.multiple_of(t * tq, tq)
        za = za_ref[0, pl.ds(start, tq), :].astype(F32)
        o_ref[0, pl.ds(start, tq), :] = (o * (za * _sigmoid(za))).astype(BF16)

    @pl.when(bounded)
    def _():
        for t in range(n_tiles):
            acc = None
            for c in range((t + 1) * nc):
                s = scores(t, c)
                if c >= t * nc:
                    s = s + bias_ref[c - t * nc]
                pv = jnp.dot(vt_ref[c], jnp.exp2(s).astype(BF16),
                             preferred_element_type=F32)
                acc = pv if acc is None else acc + pv
            finalize(t, acc)

    @pl.when(jnp.logical_not(bounded))
    def _():
        def step(t, pair, first):
            for u in range(nc):
                s = scores(t, pair * nc + u)
                s_ref[u] = s + bias_ref[u] if first else s
            m_blk = functools.reduce(
                jnp.maximum, [jnp.max(s_ref[u], axis=0, keepdims=True) for u in range(nc)])
            if first:
                m_new = m_blk
            else:
                m_new = jnp.maximum(m_ref[...], m_blk)
                alpha = jnp.exp2(m_ref[...] - m_new)
            m_ref[...] = m_new
            pv = None
            for u in range(nc):
                d = jnp.dot(vt_ref[pair * nc + u], jnp.exp2(s_ref[u] - m_new).astype(BF16),
                            preferred_element_type=F32)
                pv = d if pv is None else pv + d
            acc_ref[...] = pv if first else alpha * acc_ref[...] + pv

        def tile_body(t, _):
            step(t, t, True)

            def rest(pair, _):
                step(t, pair, False)
                return 0
            lax.fori_loop(0, t, rest, 0)
            finalize(t, acc_ref[...])
            return 0
        lax.fori_loop(0, n_tiles, tile_body, 0)


def _attn_branch(z3, lq1, lk1, lq2, lk2, subln_g, lam_init):
    n_b, seq, _ = z3.shape
    small = lambda n: pl.BlockSpec((1, n), lambda b, h: (0, 0))
    head = lambda col: pl.BlockSpec((1, seq, LANES), lambda b, h: (b, 0, col + h))
    return pl.pallas_call(
        functools.partial(_attn_kernel, lam_init=lam_init),
        grid=(n_b, N_HEADS),
        in_specs=[
            small(HEAD_DIM), small(HEAD_DIM), small(HEAD_DIM), small(HEAD_DIM),
            small(V_DIM),
            head(COL_Q), head(COL_K), head(COL_V), head(COL_ZA),
        ],
        out_specs=pl.BlockSpec((1, seq, LANES), lambda b, h: (b, 0, h)),
        out_shape=jax.ShapeDtypeStruct((n_b, seq, N_HEADS * V_DIM), BF16),
        scratch_shapes=[
            pltpu.VMEM((seq // ATT_TQ, LANES, 2 * ATT_TQ), BF16),
            pltpu.VMEM((seq // ATT_KC, ATT_VROWS, ATT_KC), BF16),
            pltpu.VMEM((ATT_NC, ATT_KC, 2 * ATT_TQ), F32),
            pltpu.VMEM((ATT_NC, ATT_KC, 2 * ATT_TQ), F32),
            pltpu.VMEM((1, 2 * ATT_TQ), F32),
            pltpu.VMEM((ATT_VROWS, 2 * ATT_TQ), F32),
        ],
        compiler_params=pltpu.CompilerParams(
            dimension_semantics=("arbitrary", "arbitrary"),
            vmem_limit_bytes=VMEM_LIMIT),
        name="diff_attn",
    )(lq1, lk1, lq2, lk2, subln_g, z3, z3, z3, z3)


def _merge_kernel(x_ref, yr_ref, ya_ref, gm_ref, wr_ref, wa_ref, wo_ref, pg_ref, o_ref):
    pr = jnp.dot(yr_ref[...], wr_ref[...], preferred_element_type=F32)
    pa = jnp.dot(ya_ref[...], wa_ref[...], preferred_element_type=F32)
    g = _sigmoid(gm_ref[...].astype(F32))
    m = g[:, :D_MODEL] * pr + g[:, D_MODEL:] * pa
    y = jnp.dot(m.astype(BF16), wo_ref[...], preferred_element_type=F32)
    y = y * lax.rsqrt(jnp.mean(y * y, axis=-1, keepdims=True) + NORM_EPS)
    o_ref[...] = x_ref[...] + y * pg_ref[...]


def _merge(x2, yr, ya, z2, wr, wa, wo, post_g, tm=512):
    n_tok = x2.shape[0]
    full = lambda r, c: pl.BlockSpec((r, c), lambda i: (0, 0))
    return pl.pallas_call(
        _merge_kernel,
        grid=(n_tok // tm,),
        in_specs=[
            pl.BlockSpec((tm, D_MODEL), lambda i: (i, 0)),
            pl.BlockSpec((tm, D_MODEL), lambda i: (i, 0)),
            pl.BlockSpec((tm, D_MODEL), lambda i: (i, 0)),
            pl.BlockSpec((tm, 2 * D_MODEL), lambda i: (i, COL_GM * LANES // (2 * D_MODEL))),
            full(D_MODEL, D_MODEL), full(D_MODEL, D_MODEL), full(D_MODEL, D_MODEL),
            full(1, D_MODEL),
        ],
        out_specs=pl.BlockSpec((tm, D_MODEL), lambda i: (i, 0)),
        out_shape=jax.ShapeDtypeStruct((n_tok, D_MODEL), F32),
        compiler_params=pltpu.CompilerParams(
            dimension_semantics=("arbitrary",),
            vmem_limit_bytes=VMEM_LIMIT),
        name="merge",
    )(x2, yr, ya, z2, wr, wa, wo, post_g)


def _q_column_scale():
    col = np.ones((1, D_IN_TOTAL), np.float32)
    col[:, COL_Q * LANES:COL_K * LANES] = HEAD_DIM ** -0.5 * math.log2(math.e)
    return jnp.asarray(col)


def _block_diag_gate_weights(wa, wx):
    per = RNN_CB // RNN_BLOCK

    def expand(w):
        w = w.reshape(D_MODEL // RNN_CB, per, RNN_BLOCK, RNN_BLOCK)
        eye = jnp.eye(per, dtype=w.dtype)
        return jnp.einsum('cgij,gh->cgihj', w, eye).reshape(
            D_MODEL // RNN_CB, RNN_CB, RNN_CB)

    return jnp.concatenate([expand(wa), expand(wx)], axis=-1)


def kernel(x, pre_g, post_g, w_in, conv_w, conv_b, lru_wa, lru_ba, lru_wx, lru_bx, lru_a,
           attn_lq1, attn_lk1, attn_lq2, attn_lk2, subln_g, w_br_rnn, w_br_attn, w_out):
    n_b, seq, d = x.shape
    depth = pre_g.shape[0]
    h2 = x.reshape(n_b * seq, d)
    n_c = D_MODEL // RNN_CB
    for l in range(depth):
        lam_init = 0.8 - 0.6 * math.exp(-0.3 * l)
        z2 = _in_proj(h2, pre_g[l][None, :], (w_in[l] * _q_column_scale()).astype(BF16))
        z3 = z2.reshape(n_b, seq, D_IN_TOTAL)

        wg = _block_diag_gate_weights(lru_wa[l], lru_wx[l]).astype(BF16)
        bg = jnp.concatenate([lru_ba[l].reshape(n_c, 1, RNN_CB),
                              lru_bx[l].reshape(n_c, 1, RNN_CB)], axis=-1)
        y_r = _rnn_branch(z3, conv_w[l], conv_b[l][None, :], wg, bg, lru_a[l][None, :])

        y_a = _attn_branch(z3, attn_lq1[l][None, :], attn_lk1[l][None, :],
                           attn_lq2[l][None, :], attn_lk2[l][None, :],
                           subln_g[l][None, :], lam_init)

        h2 = _merge(h2, y_r.reshape(n_b * seq, d), y_a.reshape(n_b * seq, d), z2,
                    w_br_rnn[l].astype(BF16), w_br_attn[l].astype(BF16),
                    w_out[l].astype(BF16), post_g[l][None, :])
    return h2.reshape(n_b, seq, d)
```

```python
import functools
import math

import jax
import jax.numpy as jnp
import numpy as np
from jax import lax
from jax.experimental import pallas as pl
from jax.experimental.pallas import tpu as pltpu

F32 = jnp.float32
BF16 = jnp.bfloat16

D_MODEL = 1024
RNN_BLOCKS = 16
RNN_BLOCK = 64
CONV_WIDTH = 4
LRU_C = 8.0
N_HEADS = 8
HEAD_DIM = 64
V_DIM = 128
NORM_EPS = 1e-6
D_IN_TOTAL = 8192

LANES = 128
COL_XR, COL_ZR, COL_Q, COL_K, COL_V, COL_ZA, COL_GM = 0, 8, 16, 24, 32, 40, 48
Z_COL0 = COL_Q

VMEM_LIMIT = 56 * 1024 * 1024


LOG2E = math.log2(math.e)


def _sigmoid(x):
    return 1.0 / (1.0 + jnp.exp2(x * (-LOG2E)))


RNN_CB = 256
RNN_TILES = D_MODEL // RNN_CB
RNN_SLABS = D_MODEL // LANES
INRNN_TB = 64
INPROJ_TN = 1024
INRNN_LEAD = 1
SQRT_FLOOR = 1e-30
Z_COLS = D_IN_TOTAL - 2 * D_MODEL


def _inproj_rnn_kernel(x_ref, g_ref, w_ref, cw_ref, cb_ref, wg_ref, bg_ref, la_ref,
                       z_ref, yr_ref, xt_ref, zr_ref, a_ref, b_ref, h_ref):
    n_b, tb = x_ref.shape[0], x_ref.shape[1]
    rows = n_b * tb
    hdr = (CONV_WIDTH - 1) * n_b

    @pl.when(pl.program_id(0) == 0)
    def _():
        xt_ref[:, 0:hdr, :] = jnp.zeros((RNN_SLABS, hdr, LANES), F32)
        h_ref[...] = jnp.zeros_like(h_ref)

    x = x_ref[...].reshape(rows, D_MODEL)
    ms = jnp.mean(x * x, axis=-1, keepdims=True)
    u = (x * lax.rsqrt(ms + NORM_EPS) * g_ref[...]).astype(BF16)

    xr = jnp.dot(u, w_ref[:, 0:D_MODEL], preferred_element_type=F32)
    for b in range(n_b):
        for s in range(RNN_SLABS):
            xt_ref[s, pl.ds(hdr + b, tb, stride=n_b), :] = (
                xr[b * tb:(b + 1) * tb, s * LANES:(s + 1) * LANES])
    zr_ref[...] = jnp.dot(u, w_ref[:, D_MODEL:2 * D_MODEL], preferred_element_type=F32)

    def project(j):
        cols = slice(j * INPROJ_TN, (j + 1) * INPROJ_TN)
        wcols = slice(2 * D_MODEL + j * INPROJ_TN, 2 * D_MODEL + (j + 1) * INPROJ_TN)
        z_ref[:, :, cols] = jnp.dot(u, w_ref[:, wcols], preferred_element_type=F32
                                    ).astype(BF16).reshape(n_b, tb, INPROJ_TN)

    n_proj = Z_COLS // INPROJ_TN
    for j in range(INRNN_LEAD):
        project(j)

    la = la_ref[...]
    log2a_scale = (-LRU_C * LOG2E) * (jnp.maximum(-la, 0.0)
                                      + jnp.log1p(jnp.exp(-jnp.abs(la))))
    per = RNN_CB // LANES
    for ct in range(RNN_TILES):
        ch = slice(ct * RNN_CB, (ct + 1) * RNN_CB)
        xc = cb_ref[:, ch]
        for tap in range(CONV_WIDTH):
            off = hdr - (CONV_WIDTH - 1 - tap) * n_b
            x_tap = jnp.concatenate(
                [xt_ref[ct * per + s, off:off + rows, :] for s in range(per)], axis=1)
            xc = xc + cw_ref[tap:tap + 1, ch] * x_tap
        gates = jnp.dot(xc.astype(BF16), wg_ref[ct], preferred_element_type=F32) + bg_ref[ct]
        sg = _sigmoid(gates)
        a = jnp.exp2(log2a_scale[:, ch] * sg[:, :RNN_CB])
        y = 1.0 - a * a
        bb = (y * lax.rsqrt(jnp.maximum(y, SQRT_FLOOR))) * (sg[:, RNN_CB:] * xc)
        for s in range(per):
            a_ref[ct * per + s] = a[:, s * LANES:(s + 1) * LANES]
            b_ref[ct * per + s] = bb[:, s * LANES:(s + 1) * LANES]
        project(INRNN_LEAD + ct)

    for s in range(RNN_SLABS):
        xt_ref[s, 0:hdr, :] = xt_ref[s, rows:rows + hdr, :]

    for j in range(INRNN_LEAD + RNN_TILES, n_proj):
        project(j)

    hs = [h_ref[s] for s in range(RNN_SLABS)]
    for t in range(tb):
        for s in range(RNN_SLABS):
            hs[s] = a_ref[s, t * n_b:(t + 1) * n_b, :] * hs[s] + b_ref[s, t * n_b:(t + 1) * n_b, :]
            b_ref[s, t * n_b:(t + 1) * n_b, :] = hs[s]
    for s in range(RNN_SLABS):
        h_ref[s] = hs[s]

    for b in range(n_b):
        h = jnp.concatenate(
            [b_ref[s, pl.ds(b, tb, stride=n_b), :] for s in range(RNN_SLABS)], axis=1)
        zr = zr_ref[b * tb:(b + 1) * tb, :]
        yr_ref[b] = (h * (zr * _sigmoid(zr))).astype(BF16)


def _in_proj_rnn(x3, g, w, conv_w, conv_b, wg, bg, lru_a):
    n_b, seq, _ = x3.shape
    rows = n_b * INRNN_TB
    const = lambda *shape: pl.BlockSpec(shape, lambda i: (0,) * len(shape))
    return pl.pallas_call(
        _inproj_rnn_kernel,
        grid=(seq // INRNN_TB,),
        in_specs=[
            pl.BlockSpec((n_b, INRNN_TB, D_MODEL), lambda i: (0, i, 0)),
            const(1, D_MODEL),
            pl.BlockSpec((D_MODEL, D_IN_TOTAL), lambda i: (0, 0), pipeline_mode=pl.Buffered(1)),
            const(CONV_WIDTH, D_MODEL), const(1, D_MODEL),
            const(RNN_TILES, RNN_CB, 2 * RNN_CB), const(RNN_TILES, 1, 2 * RNN_CB),
            const(1, D_MODEL),
        ],
        out_specs=[
            pl.BlockSpec((n_b, INRNN_TB, Z_COLS), lambda i: (0, i, 0)),
            pl.BlockSpec((n_b, INRNN_TB, D_MODEL), lambda i: (0, i, 0)),
        ],
        out_shape=[
            jax.ShapeDtypeStruct((n_b, seq, Z_COLS), BF16),
            jax.ShapeDtypeStruct((n_b, seq, D_MODEL), BF16),
        ],
        scratch_shapes=[
            pltpu.VMEM((RNN_SLABS, (CONV_WIDTH - 1) * n_b + rows, LANES), F32),
            pltpu.VMEM((rows, D_MODEL), F32),
            pltpu.VMEM((RNN_SLABS, rows, LANES), F32),
            pltpu.VMEM((RNN_SLABS, rows, LANES), F32),
            pltpu.VMEM((RNN_SLABS, n_b, LANES), F32),
        ],
        compiler_params=pltpu.CompilerParams(
            dimension_semantics=("arbitrary",),
            vmem_limit_bytes=VMEM_LIMIT),
        name="in_proj_rnn",
    )(x3, g, w, conv_w, conv_b, wg, bg, lru_a)


ATT_TQ = 512
ATT_KC = 512
ATT_NC = 1
ATT_VROWS = V_DIM + 16
ATT_BOUND = 32.0


def _attn_kernel(lq1_ref, lk1_ref, lq2_ref, lk2_ref, sg_ref, q_ref, k_ref, v_ref, za_ref,
                 o_ref, q2_ref, vt_ref, bias_ref, s_ref, m_ref, acc_ref, *, lam_init):
    tq, kc, nc = ATT_TQ, ATT_KC, ATT_NC
    assert tq == nc * kc
    n_tiles = q_ref.shape[1] // tq
    n_chunks = v_ref.shape[1] // kc

    lam = (jnp.exp(jnp.sum(lq1_ref[...] * lk1_ref[...], axis=-1, keepdims=True))
           - jnp.exp(jnp.sum(lq2_ref[...] * lk2_ref[...], axis=-1, keepdims=True))
           + lam_init)

    @pl.when((pl.program_id(0) == 0) & (pl.program_id(1) == 0))
    def _():
        row = lax.broadcasted_iota(jnp.int32, (kc, 2 * tq), 0)
        col = lax.broadcasted_iota(jnp.int32, (kc, 2 * tq), 1)
        diff = row - jnp.where(col >= tq, col - tq, col)
        for u in range(nc):
            bias_ref[u] = jnp.where(diff <= -u * kc, 0.0, -jnp.inf)

    dim = lax.broadcasted_iota(jnp.int32, (2 * HEAD_DIM, tq), 0)
    qn2 = None
    for t in range(n_tiles):
        qt = q_ref[0, t * tq:(t + 1) * tq, :].T
        zero = jnp.zeros_like(qt)
        norms = []
        for half, keep in enumerate((dim < HEAD_DIM, dim >= HEAD_DIM)):
            qh = jnp.where(keep, qt, zero)
            q2_ref[t, :, half * tq:(half + 1) * tq] = qh
            qf = qh.astype(F32)
            norms.append(jnp.sum(qf * qf, axis=0, keepdims=True))
        n2 = jnp.concatenate(norms, axis=1)
        qn2 = n2 if qn2 is None else jnp.maximum(qn2, n2)

    ones_rows = jnp.where(
        lax.broadcasted_iota(jnp.int32, (ATT_VROWS - V_DIM, kc), 0) == 0, 1.0, 0.0
    ).astype(BF16)
    for c in range(n_chunks):
        vt_ref[c, 0:V_DIM, :] = v_ref[0, c * kc:(c + 1) * kc, :].T
        vt_ref[c, V_DIM:ATT_VROWS, :] = ones_rows

    sel_dim = lax.broadcasted_iota(jnp.int32, (2 * HEAD_DIM, LANES), 0)
    sel_col = lax.broadcasted_iota(jnp.int32, (2 * HEAD_DIM, LANES), 1)
    sel = jnp.where(sel_col == jnp.where(sel_dim < HEAD_DIM, 0, 1), 1.0, 0.0).astype(BF16)
    kn2 = None
    for c in range(n_chunks):
        kf = k_ref[0, c * kc:(c + 1) * kc, :].astype(F32)
        sq = (kf * kf * (1.0 + 2.0 ** -7)).astype(BF16)
        n2 = jnp.max(jnp.dot(sq, sel, preferred_element_type=F32), axis=0, keepdims=True)
        kn2 = n2 if kn2 is None else jnp.maximum(kn2, n2)
    kcol = lax.broadcasted_iota(jnp.int32, (1, 2 * tq), 1)
    bound2 = qn2 * jnp.where(kcol < tq, kn2[:, 0:1], kn2[:, 1:2])
    bounded = jnp.max(bound2) <= ATT_BOUND * ATT_BOUND

    def scores(t, c):
        if isinstance(c, int):
            kt = k_ref[0, c * kc:(c + 1) * kc, :]
        else:
            kt = k_ref[0, pl.ds(pl.multiple_of(c * kc, kc), kc), :]
        return jnp.dot(kt, q2_ref[t], preferred_element_type=F32)

    def finalize(t, acc):
        o2 = acc[0:V_DIM] * (1.0 / acc[V_DIM:V_DIM + 1])
        o = (o2[:, :tq] - lam * o2[:, tq:]).T
        o = o * lax.rsqrt(jnp.mean(o * o, axis=-1, keepdims=True) + NORM_EPS)
        o = o * sg_ref[...] * (1.0 - lam_init)
        start = t * tq if isinstance(t, int) else pl.multiple_of(t * tq, tq)
        za = za_ref[0, pl.ds(start, tq), :].astype(F32)
        o_ref[0, pl.ds(start, tq), :] = (o * (za * _sigmoid(za))).astype(BF16)

    @pl.when(bounded)
    def _():
        for t in range(n_tiles):
            acc = None
            for c in range((t + 1) * nc):
                s = scores(t, c)
                if c >= t * nc:
                    s = s + bias_ref[c - t * nc]
                pv = jnp.dot(vt_ref[c], jnp.exp2(s).astype(BF16),
                             preferred_element_type=F32)
                acc = pv if acc is None else acc + pv
            finalize(t, acc)

    @pl.when(jnp.logical_not(bounded))
    def _():
        def step(t, pair, first):
            for u in range(nc):
                s = scores(t, pair * nc + u)
                s_ref[u] = s + bias_ref[u] if first else s
            m_blk = functools.reduce(
                jnp.maximum, [jnp.max(s_ref[u], axis=0, keepdims=True) for u in range(nc)])
            if first:
                m_new = m_blk
            else:
                m_new = jnp.maximum(m_ref[...], m_blk)
                alpha = jnp.exp2(m_ref[...] - m_new)
            m_ref[...] = m_new
            pv = None
            for u in range(nc):
                d = jnp.dot(vt_ref[pair * nc + u], jnp.exp2(s_ref[u] - m_new).astype(BF16),
                            preferred_element_type=F32)
                pv = d if pv is None else pv + d
            acc_ref[...] = pv if first else alpha * acc_ref[...] + pv

        def tile_body(t, _):
            step(t, t, True)

            def rest(pair, _):
                step(t, pair, False)
                return 0
            lax.fori_loop(0, t, rest, 0)
            finalize(t, acc_ref[...])
            return 0
        lax.fori_loop(0, n_tiles, tile_body, 0)


def _attn_branch(z3, lq1, lk1, lq2, lk2, subln_g, lam_init):
    n_b, seq, _ = z3.shape
    small = lambda n: pl.BlockSpec((1, n), lambda b, h: (0, 0))
    head = lambda col: pl.BlockSpec((1, seq, LANES), lambda b, h: (b, 0, col - Z_COL0 + h))
    return pl.pallas_call(
        functools.partial(_attn_kernel, lam_init=lam_init),
        grid=(n_b, N_HEADS),
        in_specs=[
            small(HEAD_DIM), small(HEAD_DIM), small(HEAD_DIM), small(HEAD_DIM),
            small(V_DIM),
            head(COL_Q), head(COL_K), head(COL_V), head(COL_ZA),
        ],
        out_specs=pl.BlockSpec((1, seq, LANES), lambda b, h: (b, 0, h)),
        out_shape=jax.ShapeDtypeStruct((n_b, seq, N_HEADS * V_DIM), BF16),
        scratch_shapes=[
            pltpu.VMEM((seq // ATT_TQ, LANES, 2 * ATT_TQ), BF16),
            pltpu.VMEM((seq // ATT_KC, ATT_VROWS, ATT_KC), BF16),
            pltpu.VMEM((ATT_NC, ATT_KC, 2 * ATT_TQ), F32),
            pltpu.VMEM((ATT_NC, ATT_KC, 2 * ATT_TQ), F32),
            pltpu.VMEM((1, 2 * ATT_TQ), F32),
            pltpu.VMEM((ATT_VROWS, 2 * ATT_TQ), F32),
        ],
        compiler_params=pltpu.CompilerParams(
            dimension_semantics=("arbitrary", "arbitrary"),
            vmem_limit_bytes=VMEM_LIMIT),
        name="diff_attn",
    )(lq1, lk1, lq2, lk2, subln_g, z3, z3, z3, z3)


def _merge_kernel(x_ref, yr_ref, ya_ref, gm_ref, wr_ref, wa_ref, wo_ref, pg_ref, o_ref):
    pr = jnp.dot(yr_ref[...], wr_ref[...], preferred_element_type=F32)
    pa = jnp.dot(ya_ref[...], wa_ref[...], preferred_element_type=F32)
    g = _sigmoid(gm_ref[...].astype(F32))
    m = g[:, :D_MODEL] * pr + g[:, D_MODEL:] * pa
    y = jnp.dot(m.astype(BF16), wo_ref[...], preferred_element_type=F32)
    y = y * lax.rsqrt(jnp.mean(y * y, axis=-1, keepdims=True) + NORM_EPS)
    o_ref[...] = x_ref[...] + y * pg_ref[...]


def _merge(x2, yr, ya, z2, wr, wa, wo, post_g, tm=512):
    n_tok = x2.shape[0]
    full = lambda r, c: pl.BlockSpec((r, c), lambda i: (0, 0))
    return pl.pallas_call(
        _merge_kernel,
        grid=(n_tok // tm,),
        in_specs=[
            pl.BlockSpec((tm, D_MODEL), lambda i: (i, 0)),
            pl.BlockSpec((tm, D_MODEL), lambda i: (i, 0)),
            pl.BlockSpec((tm, D_MODEL), lambda i: (i, 0)),
            pl.BlockSpec((tm, 2 * D_MODEL),
                         lambda i: (i, (COL_GM - Z_COL0) * LANES // (2 * D_MODEL))),
            full(D_MODEL, D_MODEL), full(D_MODEL, D_MODEL), full(D_MODEL, D_MODEL),
            full(1, D_MODEL),
        ],
        out_specs=pl.BlockSpec((tm, D_MODEL), lambda i: (i, 0)),
        out_shape=jax.ShapeDtypeStruct((n_tok, D_MODEL), F32),
        compiler_params=pltpu.CompilerParams(
            dimension_semantics=("arbitrary",),
            vmem_limit_bytes=VMEM_LIMIT),
        name="merge",
    )(x2, yr, ya, z2, wr, wa, wo, post_g)


def _q_column_scale():
    col = np.ones((1, D_IN_TOTAL), np.float32)
    col[:, COL_Q * LANES:COL_K * LANES] = HEAD_DIM ** -0.5 * math.log2(math.e)
    return jnp.asarray(col)


def _block_diag_gate_weights(wa, wx):
    per = RNN_CB // RNN_BLOCK

    def expand(w):
        w = w.reshape(D_MODEL // RNN_CB, per, RNN_BLOCK, RNN_BLOCK)
        eye = jnp.eye(per, dtype=w.dtype)
        return jnp.einsum('cgij,gh->cgihj', w, eye).reshape(
            D_MODEL // RNN_CB, RNN_CB, RNN_CB)

    return jnp.concatenate([expand(wa), expand(wx)], axis=-1)


def kernel(x, pre_g, post_g, w_in, conv_w, conv_b, lru_wa, lru_ba, lru_wx, lru_bx, lru_a,
           attn_lq1, attn_lk1, attn_lq2, attn_lk2, subln_g, w_br_rnn, w_br_attn, w_out):
    n_b, seq, d = x.shape
    depth = pre_g.shape[0]
    h3 = x
    for l in range(depth):
        lam_init = 0.8 - 0.6 * math.exp(-0.3 * l)
        wg = _block_diag_gate_weights(lru_wa[l], lru_wx[l]).astype(BF16)
        bg = jnp.concatenate([lru_ba[l].reshape(RNN_TILES, 1, RNN_CB),
                              lru_bx[l].reshape(RNN_TILES, 1, RNN_CB)], axis=-1)
        z3, y_r = _in_proj_rnn(h3, pre_g[l][None, :],
                               (w_in[l] * _q_column_scale()).astype(BF16),
                               conv_w[l], conv_b[l][None, :], wg, bg, lru_a[l][None, :])

        y_a = _attn_branch(z3, attn_lq1[l][None, :], attn_lk1[l][None, :],
                           attn_lq2[l][None, :], attn_lk2[l][None, :],
                           subln_g[l][None, :], lam_init)

        h2 = _merge(h3.reshape(n_b * seq, d), y_r.reshape(n_b * seq, d),
                    y_a.reshape(n_b * seq, d), z3.reshape(n_b * seq, Z_COLS),
                    w_br_rnn[l].astype(BF16), w_br_attn[l].astype(BF16),
                    w_out[l].astype(BF16), post_g[l][None, :])
        h3 = h2.reshape(n_b, seq, d)
    return h3
```

```python
import functools
import math

import jax
import jax.numpy as jnp
import numpy as np
from jax import lax
from jax.experimental import pallas as pl
from jax.experimental.pallas import tpu as pltpu

F32 = jnp.float32
BF16 = jnp.bfloat16

D_MODEL = 1024
RNN_BLOCKS = 16
RNN_BLOCK = 64
CONV_WIDTH = 4
LRU_C = 8.0
N_HEADS = 8
HEAD_DIM = 64
V_DIM = 128
NORM_EPS = 1e-6
D_IN_TOTAL = 8192

LANES = 128
COL_XR, COL_ZR, COL_Q, COL_K, COL_V, COL_ZA, COL_GM = 0, 8, 16, 24, 32, 40, 48
Z_COL0 = COL_Q

VMEM_LIMIT = 56 * 1024 * 1024


LOG2E = math.log2(math.e)


def _sigmoid(x):
    return 1.0 / (1.0 + jnp.exp2(x * (-LOG2E)))


RNN_CB = 256
RNN_TILES = D_MODEL // RNN_CB
RNN_SLABS = D_MODEL // LANES
INRNN_TB = 64
INPROJ_TN = 512
INRNN_LEAD = 2
SQRT_FLOOR = 1e-30
Z_COLS = D_IN_TOTAL - 2 * D_MODEL


def _inproj_rnn_kernel(x_ref, g_ref, w_ref, cw_ref, cb_ref, wg_ref, bg_ref, la_ref,
                       z_ref, yr_ref, xt_ref, zr_ref, a_ref, b_ref, h_ref):
    n_b, tb = x_ref.shape[0], x_ref.shape[1]
    rows = n_b * tb
    hdr = (CONV_WIDTH - 1) * n_b

    @pl.when(pl.program_id(0) == 0)
    def _():
        xt_ref[:, 0:hdr, :] = jnp.zeros((RNN_SLABS, hdr, LANES), F32)
        h_ref[...] = jnp.zeros_like(h_ref)

    x = x_ref[...].reshape(rows, D_MODEL)
    ms = jnp.mean(x * x, axis=-1, keepdims=True)
    u = (x * lax.rsqrt(ms + NORM_EPS) * g_ref[...]).astype(BF16)

    xr = jnp.dot(u, w_ref[:, 0:D_MODEL], preferred_element_type=F32)
    for b in range(n_b):
        for s in range(RNN_SLABS):
            xt_ref[s, pl.ds(hdr + b, tb, stride=n_b), :] = (
                xr[b * tb:(b + 1) * tb, s * LANES:(s + 1) * LANES])
    zr_ref[...] = jnp.dot(u, w_ref[:, D_MODEL:2 * D_MODEL], preferred_element_type=F32)

    def project(j):
        cols = slice(j * INPROJ_TN, (j + 1) * INPROJ_TN)
        wcols = slice(2 * D_MODEL + j * INPROJ_TN, 2 * D_MODEL + (j + 1) * INPROJ_TN)
        z_ref[:, :, cols] = jnp.dot(u, w_ref[:, wcols], preferred_element_type=F32
                                    ).astype(BF16).reshape(n_b, tb, INPROJ_TN)

    n_proj = Z_COLS // INPROJ_TN
    for j in range(INRNN_LEAD):
        project(j)

    la = la_ref[...]
    log2a_scale = (-LRU_C * LOG2E) * (jnp.maximum(-la, 0.0)
                                      + jnp.log1p(jnp.exp(-jnp.abs(la))))
    per = RNN_CB // LANES
    for ct in range(RNN_TILES):
        ch = slice(ct * RNN_CB, (ct + 1) * RNN_CB)
        xc = cb_ref[:, ch]
        for tap in range(CONV_WIDTH):
            off = hdr - (CONV_WIDTH - 1 - tap) * n_b
            x_tap = jnp.concatenate(
                [xt_ref[ct * per + s, off:off + rows, :] for s in range(per)], axis=1)
            xc = xc + cw_ref[tap:tap + 1, ch] * x_tap
        xcb = xc.astype(BF16)
        sg = []
        for gate in range(2):
            gcols = slice(gate * RNN_CB, (gate + 1) * RNN_CB)
            pre = jnp.dot(xcb, wg_ref[ct, :, gcols], preferred_element_type=F32)
            sg.append(_sigmoid(pre + bg_ref[ct, :, gcols]))
            project(INRNN_LEAD + ct * 2 + gate)
        a = jnp.exp2(log2a_scale[:, ch] * sg[0])
        y = 1.0 - a * a
        bb = (y * lax.rsqrt(jnp.maximum(y, SQRT_FLOOR))) * (sg[1] * xc)
        for s in range(per):
            a_ref[ct * per + s] = a[:, s * LANES:(s + 1) * LANES]
            b_ref[ct * per + s] = bb[:, s * LANES:(s + 1) * LANES]

    for s in range(RNN_SLABS):
        xt_ref[s, 0:hdr, :] = xt_ref[s, rows:rows + hdr, :]

    for j in range(INRNN_LEAD + RNN_TILES * 2, n_proj):
        project(j)

    hs = [h_ref[s] for s in range(RNN_SLABS)]
    for t in range(tb):
        for s in range(RNN_SLABS):
            hs[s] = a_ref[s, t * n_b:(t + 1) * n_b, :] * hs[s] + b_ref[s, t * n_b:(t + 1) * n_b, :]
            b_ref[s, t * n_b:(t + 1) * n_b, :] = hs[s]
    for s in range(RNN_SLABS):
        h_ref[s] = hs[s]

    for b in range(n_b):
        h = jnp.concatenate(
            [b_ref[s, pl.ds(b, tb, stride=n_b), :] for s in range(RNN_SLABS)], axis=1)
        zr = zr_ref[b * tb:(b + 1) * tb, :]
        yr_ref[b] = (h * (zr * _sigmoid(zr))).astype(BF16)


def _in_proj_rnn(x3, g, w, conv_w, conv_b, wg, bg, lru_a):
    n_b, seq, _ = x3.shape
    rows = n_b * INRNN_TB
    const = lambda *shape: pl.BlockSpec(shape, lambda i: (0,) * len(shape))
    return pl.pallas_call(
        _inproj_rnn_kernel,
        grid=(seq // INRNN_TB,),
        in_specs=[
            pl.BlockSpec((n_b, INRNN_TB, D_MODEL), lambda i: (0, i, 0)),
            const(1, D_MODEL),
            pl.BlockSpec((D_MODEL, D_IN_TOTAL), lambda i: (0, 0), pipeline_mode=pl.Buffered(1)),
            const(CONV_WIDTH, D_MODEL), const(1, D_MODEL),
            const(RNN_TILES, RNN_CB, 2 * RNN_CB), const(RNN_TILES, 1, 2 * RNN_CB),
            const(1, D_MODEL),
        ],
        out_specs=[
            pl.BlockSpec((n_b, INRNN_TB, Z_COLS), lambda i: (0, i, 0)),
            pl.BlockSpec((n_b, INRNN_TB, D_MODEL), lambda i: (0, i, 0)),
        ],
        out_shape=[
            jax.ShapeDtypeStruct((n_b, seq, Z_COLS), BF16),
            jax.ShapeDtypeStruct((n_b, seq, D_MODEL), BF16),
        ],
        scratch_shapes=[
            pltpu.VMEM((RNN_SLABS, (CONV_WIDTH - 1) * n_b + rows, LANES), F32),
            pltpu.VMEM((rows, D_MODEL), F32),
            pltpu.VMEM((RNN_SLABS, rows, LANES), F32),
            pltpu.VMEM((RNN_SLABS, rows, LANES), F32),
            pltpu.VMEM((RNN_SLABS, n_b, LANES), F32),
        ],
        compiler_params=pltpu.CompilerParams(
            dimension_semantics=("arbitrary",),
            vmem_limit_bytes=VMEM_LIMIT),
        name="in_proj_rnn",
    )(x3, g, w, conv_w, conv_b, wg, bg, lru_a)


ATT_TQ = 512
ATT_KC = 512
ATT_NC = 1
ATT_VROWS = V_DIM + 16
ATT_BOUND = 32.0


def _attn_kernel(lq1_ref, lk1_ref, lq2_ref, lk2_ref, sg_ref, q_ref, k_ref, v_ref, za_ref,
                 o_ref, q2_ref, vt_ref, bias_ref, s_ref, m_ref, acc_ref, *, lam_init):
    tq, kc, nc = ATT_TQ, ATT_KC, ATT_NC
    assert tq == nc * kc
    n_tiles = q_ref.shape[1] // tq
    n_chunks = v_ref.shape[1] // kc

    lam = (jnp.exp(jnp.sum(lq1_ref[...] * lk1_ref[...], axis=-1, keepdims=True))
           - jnp.exp(jnp.sum(lq2_ref[...] * lk2_ref[...], axis=-1, keepdims=True))
           + lam_init)

    @pl.when((pl.program_id(0) == 0) & (pl.program_id(1) == 0))
    def _():
        row = lax.broadcasted_iota(jnp.int32, (kc, 2 * tq), 0)
        col = lax.broadcasted_iota(jnp.int32, (kc, 2 * tq), 1)
        diff = row - jnp.where(col >= tq, col - tq, col)
        for u in range(nc):
            bias_ref[u] = jnp.where(diff <= -u * kc, 0.0, -jnp.inf)

    dim = lax.broadcasted_iota(jnp.int32, (2 * HEAD_DIM, tq), 0)
    qn2 = None
    for t in range(n_tiles):
        qt = q_ref[0, t * tq:(t + 1) * tq, :].T
        zero = jnp.zeros_like(qt)
        norms = []
        for half, keep in enumerate((dim < HEAD_DIM, dim >= HEAD_DIM)):
            qh = jnp.where(keep, qt, zero)
            q2_ref[t, :, half * tq:(half + 1) * tq] = qh
            qf = qh.astype(F32)
            norms.append(jnp.sum(qf * qf, axis=0, keepdims=True))
        n2 = jnp.concatenate(norms, axis=1)
        qn2 = n2 if qn2 is None else jnp.maximum(qn2, n2)

    ones_rows = jnp.where(
        lax.broadcasted_iota(jnp.int32, (ATT_VROWS - V_DIM, kc), 0) == 0, 1.0, 0.0
    ).astype(BF16)
    for c in range(n_chunks):
        vt_ref[c, 0:V_DIM, :] = v_ref[0, c * kc:(c + 1) * kc, :].T
        vt_ref[c, V_DIM:ATT_VROWS, :] = ones_rows

    sel_dim = lax.broadcasted_iota(jnp.int32, (2 * HEAD_DIM, LANES), 0)
    sel_col = lax.broadcasted_iota(jnp.int32, (2 * HEAD_DIM, LANES), 1)
    sel = jnp.where(sel_col == jnp.where(sel_dim < HEAD_DIM, 0, 1), 1.0, 0.0).astype(BF16)
    kn2 = None
    for c in range(n_chunks):
        kf = k_ref[0, c * kc:(c + 1) * kc, :].astype(F32)
        sq = (kf * kf * (1.0 + 2.0 ** -7)).astype(BF16)
        n2 = jnp.max(jnp.dot(sq, sel, preferred_element_type=F32), axis=0, keepdims=True)
        kn2 = n2 if kn2 is None else jnp.maximum(kn2, n2)
    kcol = lax.broadcasted_iota(jnp.int32, (1, 2 * tq), 1)
    bound2 = qn2 * jnp.where(kcol < tq, kn2[:, 0:1], kn2[:, 1:2])
    bounded = jnp.max(bound2) <= ATT_BOUND * ATT_BOUND

    def scores(t, c):
        if isinstance(c, int):
            kt = k_ref[0, c * kc:(c + 1) * kc, :]
        else:
            kt = k_ref[0, pl.ds(pl.multiple_of(c * kc, kc), kc), :]
        return jnp.dot(kt, q2_ref[t], preferred_element_type=F32)

    def finalize(t, acc):
        o2 = acc[0:V_DIM] * (1.0 / acc[V_DIM:V_DIM + 1])
        o = (o2[:, :tq] - lam * o2[:, tq:]).T
        o = o * lax.rsqrt(jnp.mean(o * o, axis=-1, keepdims=True) + NORM_EPS)
        o = o * sg_ref[...] * (1.0 - lam_init)
        start = t * tq if isinstance(t, int) else pl.multiple_of(t * tq, tq)
        za = za_ref[0, pl.ds(start, tq), :].astype(F32)
        o_ref[0, pl.ds(start, tq), :] = (o * (za * _sigmoid(za))).astype(BF16)

    for t in range(n_tiles):
        acc = None
        for c in range((t + 1) * nc):
            s = scores(t, c)
            if c >= t * nc:
                s = s + bias_ref[c - t * nc]
            pv = jnp.dot(vt_ref[c], jnp.exp2(s).astype(BF16),
                         preferred_element_type=F32)
            acc = pv if acc is None else acc + pv
        finalize(t, acc)

    @pl.when(jnp.logical_not(bounded))
    def _():
        def step(t, pair, first):
            for u in range(nc):
                s = scores(t, pair * nc + u)
                s_ref[u] = s + bias_ref[u] if first else s
            m_blk = functools.reduce(
                jnp.maximum, [jnp.max(s_ref[u], axis=0, keepdims=True) for u in range(nc)])
            if first:
                m_new = m_blk
            else:
                m_new = jnp.maximum(m_ref[...], m_blk)
                alpha = jnp.exp2(m_ref[...] - m_new)
            m_ref[...] = m_new
            pv = None
            for u in range(nc):
                d = jnp.dot(vt_ref[pair * nc + u], jnp.exp2(s_ref[u] - m_new).astype(BF16),
                            preferred_element_type=F32)
                pv = d if pv is None else pv + d
            acc_ref[...] = pv if first else alpha * acc_ref[...] + pv

        def tile_body(t, _):
            step(t, t, True)

            def rest(pair, _):
                step(t, pair, False)
                return 0
            lax.fori_loop(0, t, rest, 0)
            finalize(t, acc_ref[...])
            return 0
        lax.fori_loop(0, n_tiles, tile_body, 0)


def _attn_branch(z3, lq1, lk1, lq2, lk2, subln_g, lam_init):
    n_b, seq, _ = z3.shape
    small = lambda n: pl.BlockSpec((1, n), lambda b, h: (0, 0))
    head = lambda col: pl.BlockSpec((1, seq, LANES), lambda b, h: (b, 0, col - Z_COL0 + h))
    return pl.pallas_call(
        functools.partial(_attn_kernel, lam_init=lam_init),
        grid=(n_b, N_HEADS),
        in_specs=[
            small(HEAD_DIM), small(HEAD_DIM), small(HEAD_DIM), small(HEAD_DIM),
            small(V_DIM),
            head(COL_Q), head(COL_K), head(COL_V), head(COL_ZA),
        ],
        out_specs=pl.BlockSpec((1, seq, LANES), lambda b, h: (b, 0, h)),
        out_shape=jax.ShapeDtypeStruct((n_b, seq, N_HEADS * V_DIM), BF16),
        scratch_shapes=[
            pltpu.VMEM((seq // ATT_TQ, LANES, 2 * ATT_TQ), BF16),
            pltpu.VMEM((seq // ATT_KC, ATT_VROWS, ATT_KC), BF16),
            pltpu.VMEM((ATT_NC, ATT_KC, 2 * ATT_TQ), F32),
            pltpu.VMEM((ATT_NC, ATT_KC, 2 * ATT_TQ), F32),
            pltpu.VMEM((1, 2 * ATT_TQ), F32),
            pltpu.VMEM((ATT_VROWS, 2 * ATT_TQ), F32),
        ],
        compiler_params=pltpu.CompilerParams(
            dimension_semantics=("arbitrary", "arbitrary"),
            vmem_limit_bytes=VMEM_LIMIT),
        name="diff_attn",
    )(lq1, lk1, lq2, lk2, subln_g, z3, z3, z3, z3)


def _merge_kernel(x_ref, yr_ref, ya_ref, gm_ref, wr_ref, wa_ref, wo_ref, pg_ref, o_ref):
    pr = jnp.dot(yr_ref[...], wr_ref[...], preferred_element_type=F32)
    pa = jnp.dot(ya_ref[...], wa_ref[...], preferred_element_type=F32)
    g = _sigmoid(gm_ref[...].astype(F32))
    m = g[:, :D_MODEL] * pr + g[:, D_MODEL:] * pa
    y = jnp.dot(m.astype(BF16), wo_ref[...], preferred_element_type=F32)
    y = y * lax.rsqrt(jnp.mean(y * y, axis=-1, keepdims=True) + NORM_EPS)
    o_ref[...] = x_ref[...] + y * pg_ref[...]


def _merge(x2, yr, ya, z2, wr, wa, wo, post_g, tm=512):
    n_tok = x2.shape[0]
    full = lambda r, c: pl.BlockSpec((r, c), lambda i: (0, 0))
    return pl.pallas_call(
        _merge_kernel,
        grid=(n_tok // tm,),
        in_specs=[
            pl.BlockSpec((tm, D_MODEL), lambda i: (i, 0)),
            pl.BlockSpec((tm, D_MODEL), lambda i: (i, 0)),
            pl.BlockSpec((tm, D_MODEL), lambda i: (i, 0)),
            pl.BlockSpec((tm, 2 * D_MODEL),
                         lambda i: (i, (COL_GM - Z_COL0) * LANES // (2 * D_MODEL))),
            full(D_MODEL, D_MODEL), full(D_MODEL, D_MODEL), full(D_MODEL, D_MODEL),
            full(1, D_MODEL),
        ],
        out_specs=pl.BlockSpec((tm, D_MODEL), lambda i: (i, 0)),
        out_shape=jax.ShapeDtypeStruct((n_tok, D_MODEL), F32),
        compiler_params=pltpu.CompilerParams(
            dimension_semantics=("arbitrary",),
            vmem_limit_bytes=VMEM_LIMIT),
        name="merge",
    )(x2, yr, ya, z2, wr, wa, wo, post_g)


def _q_column_scale():
    col = np.ones((1, D_IN_TOTAL), np.float32)
    col[:, COL_Q * LANES:COL_K * LANES] = HEAD_DIM ** -0.5 * math.log2(math.e)
    return jnp.asarray(col)


def _block_diag_gate_weights(wa, wx):
    per = RNN_CB // RNN_BLOCK

    def expand(w):
        w = w.reshape(D_MODEL // RNN_CB, per, RNN_BLOCK, RNN_BLOCK)
        eye = jnp.eye(per, dtype=w.dtype)
        return jnp.einsum('cgij,gh->cgihj', w, eye).reshape(
            D_MODEL // RNN_CB, RNN_CB, RNN_CB)

    return jnp.concatenate([expand(wa), expand(wx)], axis=-1)


def kernel(x, pre_g, post_g, w_in, conv_w, conv_b, lru_wa, lru_ba, lru_wx, lru_bx, lru_a,
           attn_lq1, attn_lk1, attn_lq2, attn_lk2, subln_g, w_br_rnn, w_br_attn, w_out):
    n_b, seq, d = x.shape
    depth = pre_g.shape[0]
    h3 = x
    for l in range(depth):
        lam_init = 0.8 - 0.6 * math.exp(-0.3 * l)
        wg = _block_diag_gate_weights(lru_wa[l], lru_wx[l]).astype(BF16)
        bg = jnp.concatenate([lru_ba[l].reshape(RNN_TILES, 1, RNN_CB),
                              lru_bx[l].reshape(RNN_TILES, 1, RNN_CB)], axis=-1)
        z3, y_r = _in_proj_rnn(h3, pre_g[l][None, :],
                               (w_in[l] * _q_column_scale()).astype(BF16),
                               conv_w[l], conv_b[l][None, :], wg, bg, lru_a[l][None, :])

        y_a = _attn_branch(z3, attn_lq1[l][None, :], attn_lk1[l][None, :],
                           attn_lq2[l][None, :], attn_lk2[l][None, :],
                           subln_g[l][None, :], lam_init)

        h2 = _merge(h3.reshape(n_b * seq, d), y_r.reshape(n_b * seq, d),
                    y_a.reshape(n_b * seq, d), z3.reshape(n_b * seq, Z_COLS),
                    w_br_rnn[l].astype(BF16), w_br_attn[l].astype(BF16),
                    w_out[l].astype(BF16), post_g[l][None, :])
        h3 = h2.reshape(n_b, seq, d)
    return h3
```

```python
import functools
import math

import jax
import jax.numpy as jnp
import numpy as np
from jax import lax
from jax.experimental import pallas as pl
from jax.experimental.pallas import tpu as pltpu

F32 = jnp.float32
BF16 = jnp.bfloat16

D_MODEL = 1024
RNN_BLOCKS = 16
RNN_BLOCK = 64
CONV_WIDTH = 4
LRU_C = 8.0
N_HEADS = 8
HEAD_DIM = 64
V_DIM = 128
NORM_EPS = 1e-6
D_IN_TOTAL = 8192

LANES = 128
COL_XR, COL_ZR, COL_Q, COL_K, COL_V, COL_ZA, COL_GM = 0, 8, 16, 24, 32, 40, 48
Z_COL0 = COL_Q

VMEM_LIMIT = 56 * 1024 * 1024


LOG2E = math.log2(math.e)


def _sigmoid(x):
    return 1.0 / (1.0 + jnp.exp2(x * (-LOG2E)))


RNN_CB = 256
RNN_TILES = D_MODEL // RNN_CB
RNN_SLABS = D_MODEL // LANES
INRNN_TB = 64
INPROJ_TN = 512
INRNN_LEAD = 2
SQRT_FLOOR = 1e-30
Z_COLS = D_IN_TOTAL - 2 * D_MODEL


def _inproj_rnn_kernel(x_ref, g_ref, w_ref, cw_ref, cb_ref, wg_ref, bg_ref, la_ref,
                       z_ref, yr_ref, xt_ref, zr_ref, a_ref, b_ref, h_ref):
    n_b, tb = x_ref.shape[0], x_ref.shape[1]
    rows = n_b * tb
    hdr = (CONV_WIDTH - 1) * n_b

    @pl.when(pl.program_id(0) == 0)
    def _():
        xt_ref[:, 0:hdr, :] = jnp.zeros((RNN_SLABS, hdr, LANES), F32)
        h_ref[...] = jnp.zeros_like(h_ref)

    x = x_ref[...].reshape(rows, D_MODEL)
    ms = jnp.mean(x * x, axis=-1, keepdims=True)
    u = (x * lax.rsqrt(ms + NORM_EPS) * g_ref[...]).astype(BF16)

    xr = jnp.dot(u, w_ref[:, 0:D_MODEL], preferred_element_type=F32)
    for b in range(n_b):
        for s in range(RNN_SLABS):
            xt_ref[s, pl.ds(hdr + b, tb, stride=n_b), :] = (
                xr[b * tb:(b + 1) * tb, s * LANES:(s + 1) * LANES])
    zr_ref[...] = jnp.dot(u, w_ref[:, D_MODEL:2 * D_MODEL], preferred_element_type=F32)

    def project(j):
        cols = slice(j * INPROJ_TN, (j + 1) * INPROJ_TN)
        wcols = slice(2 * D_MODEL + j * INPROJ_TN, 2 * D_MODEL + (j + 1) * INPROJ_TN)
        z_ref[:, :, cols] = jnp.dot(u, w_ref[:, wcols], preferred_element_type=F32
                                    ).astype(BF16).reshape(n_b, tb, INPROJ_TN)

    n_proj = Z_COLS // INPROJ_TN
    for j in range(INRNN_LEAD):
        project(j)

    la = la_ref[...]
    log2a_scale = (-LRU_C * LOG2E) * (jnp.maximum(-la, 0.0)
                                      + jnp.log1p(jnp.exp(-jnp.abs(la))))
    per = RNN_CB // LANES
    for ct in range(RNN_TILES):
        ch = slice(ct * RNN_CB, (ct + 1) * RNN_CB)
        xc = cb_ref[:, ch]
        for tap in range(CONV_WIDTH):
            off = hdr - (CONV_WIDTH - 1 - tap) * n_b
            x_tap = jnp.concatenate(
                [xt_ref[ct * per + s, off:off + rows, :] for s in range(per)], axis=1)
            xc = xc + cw_ref[tap:tap + 1, ch] * x_tap
        xcb = xc.astype(BF16)
        sg = []
        for gate in range(2):
            gcols = slice(gate * RNN_CB, (gate + 1) * RNN_CB)
            pre = jnp.dot(xcb, wg_ref[ct, :, gcols], preferred_element_type=F32)
            sg.append(_sigmoid(pre + bg_ref[ct, :, gcols]))
            project(INRNN_LEAD + ct * 2 + gate)
        a = jnp.exp2(log2a_scale[:, ch] * sg[0])
        y = 1.0 - a * a
        bb = (y * lax.rsqrt(jnp.maximum(y, SQRT_FLOOR))) * (sg[1] * xc)
        for s in range(per):
            a_ref[ct * per + s] = a[:, s * LANES:(s + 1) * LANES]
            b_ref[ct * per + s] = bb[:, s * LANES:(s + 1) * LANES]

    for s in range(RNN_SLABS):
        xt_ref[s, 0:hdr, :] = xt_ref[s, rows:rows + hdr, :]

    for j in range(INRNN_LEAD + RNN_TILES * 2, n_proj):
        project(j)

    hs = [h_ref[s] for s in range(RNN_SLABS)]
    for t in range(tb):
        for s in range(RNN_SLABS):
            hs[s] = a_ref[s, t * n_b:(t + 1) * n_b, :] * hs[s] + b_ref[s, t * n_b:(t + 1) * n_b, :]
            b_ref[s, t * n_b:(t + 1) * n_b, :] = hs[s]
    for s in range(RNN_SLABS):
        h_ref[s] = hs[s]

    for b in range(n_b):
        h = jnp.concatenate(
            [b_ref[s, pl.ds(b, tb, stride=n_b), :] for s in range(RNN_SLABS)], axis=1)
        zr = zr_ref[b * tb:(b + 1) * tb, :]
        yr_ref[b] = (h * (zr * _sigmoid(zr))).astype(BF16)


def _in_proj_rnn(x3, g, w, conv_w, conv_b, wg, bg, lru_a):
    n_b, seq, _ = x3.shape
    rows = n_b * INRNN_TB
    const = lambda *shape: pl.BlockSpec(shape, lambda i: (0,) * len(shape))
    return pl.pallas_call(
        _inproj_rnn_kernel,
        grid=(seq // INRNN_TB,),
        in_specs=[
            pl.BlockSpec((n_b, INRNN_TB, D_MODEL), lambda i: (0, i, 0)),
            const(1, D_MODEL),
            pl.BlockSpec((D_MODEL, D_IN_TOTAL), lambda i: (0, 0), pipeline_mode=pl.Buffered(1)),
            const(CONV_WIDTH, D_MODEL), const(1, D_MODEL),
            const(RNN_TILES, RNN_CB, 2 * RNN_CB), const(RNN_TILES, 1, 2 * RNN_CB),
            const(1, D_MODEL),
        ],
        out_specs=[
            pl.BlockSpec((n_b, INRNN_TB, Z_COLS), lambda i: (0, i, 0)),
            pl.BlockSpec((n_b, INRNN_TB, D_MODEL), lambda i: (0, i, 0)),
        ],
        out_shape=[
            jax.ShapeDtypeStruct((n_b, seq, Z_COLS), BF16),
            jax.ShapeDtypeStruct((n_b, seq, D_MODEL), BF16),
        ],
        scratch_shapes=[
            pltpu.VMEM((RNN_SLABS, (CONV_WIDTH - 1) * n_b + rows, LANES), F32),
            pltpu.VMEM((rows, D_MODEL), F32),
            pltpu.VMEM((RNN_SLABS, rows, LANES), F32),
            pltpu.VMEM((RNN_SLABS, rows, LANES), F32),
            pltpu.VMEM((RNN_SLABS, n_b, LANES), F32),
        ],
        compiler_params=pltpu.CompilerParams(
            dimension_semantics=("arbitrary",),
            vmem_limit_bytes=VMEM_LIMIT),
        name="in_proj_rnn",
    )(x3, g, w, conv_w, conv_b, wg, bg, lru_a)


ATT_TQ = 512
ATT_KC = 512
ATT_NC = 1
ATT_VROWS = V_DIM + 16
ATT_BOUND = 32.0


def _attn_kernel(lq1_ref, lk1_ref, lq2_ref, lk2_ref, sg_ref, q_ref, k_ref, v_ref, za_ref,
                 o_ref, q2_ref, vt_ref, bias_ref, s_ref, m_ref, acc_ref, *, lam_init):
    tq, kc, nc = ATT_TQ, ATT_KC, ATT_NC
    assert tq == nc * kc
    n_tiles = q_ref.shape[1] // tq
    n_chunks = v_ref.shape[1] // kc

    lam = (jnp.exp(jnp.sum(lq1_ref[...] * lk1_ref[...], axis=-1, keepdims=True))
           - jnp.exp(jnp.sum(lq2_ref[...] * lk2_ref[...], axis=-1, keepdims=True))
           + lam_init)

    @pl.when((pl.program_id(0) == 0) & (pl.program_id(1) == 0))
    def _():
        row = lax.broadcasted_iota(jnp.int32, (kc, 2 * tq), 0)
        col = lax.broadcasted_iota(jnp.int32, (kc, 2 * tq), 1)
        diff = row - jnp.where(col >= tq, col - tq, col)
        for u in range(nc):
            bias_ref[u] = jnp.where(diff <= -u * kc, 0.0, -jnp.inf)

    dim = lax.broadcasted_iota(jnp.int32, (2 * HEAD_DIM, tq), 0)
    qn2 = None
    for t in range(n_tiles):
        qt = q_ref[0, t * tq:(t + 1) * tq, :].T
        zero = jnp.zeros_like(qt)
        norms = []
        for half, keep in enumerate((dim < HEAD_DIM, dim >= HEAD_DIM)):
            qh = jnp.where(keep, qt, zero)
            q2_ref[t, :, half * tq:(half + 1) * tq] = qh
            qf = qh.astype(F32)
            norms.append(jnp.sum(qf * qf, axis=0, keepdims=True))
        n2 = jnp.concatenate(norms, axis=1)
        qn2 = n2 if qn2 is None else jnp.maximum(qn2, n2)

    ones_rows = jnp.where(
        lax.broadcasted_iota(jnp.int32, (ATT_VROWS - V_DIM, kc), 0) == 0, 1.0, 0.0
    ).astype(BF16)
    for c in range(n_chunks):
        vt_ref[c, 0:V_DIM, :] = v_ref[0, c * kc:(c + 1) * kc, :].T
        vt_ref[c, V_DIM:ATT_VROWS, :] = ones_rows

    klane = lax.broadcasted_iota(jnp.int32, (kc, 2 * HEAD_DIM), 1)
    kn2 = [None, None]
    for c in range(n_chunks):
        kf = k_ref[0, c * kc:(c + 1) * kc, :].astype(F32)
        sq = kf * kf
        for half, keep in enumerate((klane < HEAD_DIM, klane >= HEAD_DIM)):
            n2 = jnp.max(jnp.sum(jnp.where(keep, sq, 0.0), axis=1, keepdims=True),
                         axis=0, keepdims=True)
            kn2[half] = n2 if kn2[half] is None else jnp.maximum(kn2[half], n2)
    kcol = lax.broadcasted_iota(jnp.int32, (1, 2 * tq), 1)
    bound2 = qn2 * jnp.where(kcol < tq, kn2[0], kn2[1])
    bounded = jnp.max(bound2) <= ATT_BOUND * ATT_BOUND

    def scores(t, c):
        if isinstance(c, int):
            kt = k_ref[0, c * kc:(c + 1) * kc, :]
        else:
            kt = k_ref[0, pl.ds(pl.multiple_of(c * kc, kc), kc), :]
        return jnp.dot(kt, q2_ref[t], preferred_element_type=F32)

    def finalize(t, acc):
        o2 = acc[0:V_DIM] * (1.0 / acc[V_DIM:V_DIM + 1])
        o = (o2[:, :tq] - lam * o2[:, tq:]).T
        o = o * lax.rsqrt(jnp.mean(o * o, axis=-1, keepdims=True) + NORM_EPS)
        o = o * sg_ref[...] * (1.0 - lam_init)
        start = t * tq if isinstance(t, int) else pl.multiple_of(t * tq, tq)
        za = za_ref[0, pl.ds(start, tq), :].astype(F32)
        o_ref[0, pl.ds(start, tq), :] = (o * (za * _sigmoid(za))).astype(BF16)

    for t in range(n_tiles):
        acc = None
        for c in range((t + 1) * nc):
            s = scores(t, c)
            if c >= t * nc:
                s = s + bias_ref[c - t * nc]
            pv = jnp.dot(vt_ref[c], jnp.exp2(s).astype(BF16),
                         preferred_element_type=F32)
            acc = pv if acc is None else acc + pv
        finalize(t, acc)

    @pl.when(jnp.logical_not(bounded))
    def _():
        def step(t, pair, first):
            for u in range(nc):
                s = scores(t, pair * nc + u)
                s_ref[u] = s + bias_ref[u] if first else s
            m_blk = functools.reduce(
                jnp.maximum, [jnp.max(s_ref[u], axis=0, keepdims=True) for u in range(nc)])
            if first:
                m_new = m_blk
            else:
                m_new = jnp.maximum(m_ref[...], m_blk)
                alpha = jnp.exp2(m_ref[...] - m_new)
            m_ref[...] = m_new
            pv = None
            for u in range(nc):
                d = jnp.dot(vt_ref[pair * nc + u], jnp.exp2(s_ref[u] - m_new).astype(BF16),
                            preferred_element_type=F32)
                pv = d if pv is None else pv + d
            acc_ref[...] = pv if first else alpha * acc_ref[...] + pv

        def tile_body(t, _):
            step(t, t, True)

            def rest(pair, _):
                step(t, pair, False)
                return 0
            lax.fori_loop(0, t, rest, 0)
            finalize(t, acc_ref[...])
            return 0
        lax.fori_loop(0, n_tiles, tile_body, 0)


def _attn_branch(z3, lq1, lk1, lq2, lk2, subln_g, lam_init):
    n_b, seq, _ = z3.shape
    small = lambda n: pl.BlockSpec((1, n), lambda b, h: (0, 0))
    head = lambda col: pl.BlockSpec((1, seq, LANES), lambda b, h: (b, 0, col - Z_COL0 + h))
    return pl.pallas_call(
        functools.partial(_attn_kernel, lam_init=lam_init),
        grid=(n_b, N_HEADS),
        in_specs=[
            small(HEAD_DIM), small(HEAD_DIM), small(HEAD_DIM), small(HEAD_DIM),
            small(V_DIM),
            head(COL_Q), head(COL_K), head(COL_V), head(COL_ZA),
        ],
        out_specs=pl.BlockSpec((1, seq, LANES), lambda b, h: (b, 0, h)),
        out_shape=jax.ShapeDtypeStruct((n_b, seq, N_HEADS * V_DIM), BF16),
        scratch_shapes=[
            pltpu.VMEM((seq // ATT_TQ, LANES, 2 * ATT_TQ), BF16),
            pltpu.VMEM((seq // ATT_KC, ATT_VROWS, ATT_KC), BF16),
            pltpu.VMEM((ATT_NC, ATT_KC, 2 * ATT_TQ), F32),
            pltpu.VMEM((ATT_NC, ATT_KC, 2 * ATT_TQ), F32),
            pltpu.VMEM((1, 2 * ATT_TQ), F32),
            pltpu.VMEM((ATT_VROWS, 2 * ATT_TQ), F32),
        ],
        compiler_params=pltpu.CompilerParams(
            dimension_semantics=("arbitrary", "arbitrary"),
            vmem_limit_bytes=VMEM_LIMIT),
        name="diff_attn",
    )(lq1, lk1, lq2, lk2, subln_g, z3, z3, z3, z3)


MERGE_SUB = 4


def _merge_kernel(x_ref, yr_ref, ya_ref, gm_ref, wr_ref, wa_ref, wo_ref, pg_ref, o_ref):
    sub = x_ref.shape[0] // MERGE_SUB
    groups = [slice(r * sub, (r + 1) * sub) for r in range(MERGE_SUB)]
    merged = []
    for rows in groups:
        pr = jnp.dot(yr_ref[rows, :], wr_ref[...], preferred_element_type=F32)
        pa = jnp.dot(ya_ref[rows, :], wa_ref[...], preferred_element_type=F32)
        g = _sigmoid(gm_ref[rows, :].astype(F32))
        merged.append((g[:, :D_MODEL] * pr + g[:, D_MODEL:] * pa).astype(BF16))
    for rows, m in zip(groups, merged):
        y = jnp.dot(m, wo_ref[...], preferred_element_type=F32)
        y = y * lax.rsqrt(jnp.mean(y * y, axis=-1, keepdims=True) + NORM_EPS)
        o_ref[rows, :] = x_ref[rows, :] + y * pg_ref[...]


def _merge(x2, yr, ya, z2, wr, wa, wo, post_g, tm=1024):
    n_tok = x2.shape[0]
    full = lambda r, c: pl.BlockSpec((r, c), lambda i: (0, 0), pipeline_mode=pl.Buffered(1))
    return pl.pallas_call(
        _merge_kernel,
        grid=(n_tok // tm,),
        in_specs=[
            pl.BlockSpec((tm, D_MODEL), lambda i: (i, 0)),
            pl.BlockSpec((tm, D_MODEL), lambda i: (i, 0)),
            pl.BlockSpec((tm, D_MODEL), lambda i: (i, 0)),
            pl.BlockSpec((tm, 2 * D_MODEL),
                         lambda i: (i, (COL_GM - Z_COL0) * LANES // (2 * D_MODEL))),
            full(D_MODEL, D_MODEL), full(D_MODEL, D_MODEL), full(D_MODEL, D_MODEL),
            full(1, D_MODEL),
        ],
        out_specs=pl.BlockSpec((tm, D_MODEL), lambda i: (i, 0)),
        out_shape=jax.ShapeDtypeStruct((n_tok, D_MODEL), F32),
        compiler_params=pltpu.CompilerParams(
            dimension_semantics=("arbitrary",),
            vmem_limit_bytes=VMEM_LIMIT),
        name="merge",
    )(x2, yr, ya, z2, wr, wa, wo, post_g)


def _q_column_scale():
    col = np.ones((1, D_IN_TOTAL), np.float32)
    col[:, COL_Q * LANES:COL_K * LANES] = HEAD_DIM ** -0.5 * math.log2(math.e)
    return jnp.asarray(col)


def _block_diag_gate_weights(wa, wx):
    per = RNN_CB // RNN_BLOCK

    def expand(w):
        w = w.reshape(D_MODEL // RNN_CB, per, RNN_BLOCK, RNN_BLOCK)
        eye = jnp.eye(per, dtype=w.dtype)
        return jnp.einsum('cgij,gh->cgihj', w, eye).reshape(
            D_MODEL // RNN_CB, RNN_CB, RNN_CB)

    return jnp.concatenate([expand(wa), expand(wx)], axis=-1)


def kernel(x, pre_g, post_g, w_in, conv_w, conv_b, lru_wa, lru_ba, lru_wx, lru_bx, lru_a,
           attn_lq1, attn_lk1, attn_lq2, attn_lk2, subln_g, w_br_rnn, w_br_attn, w_out):
    n_b, seq, d = x.shape
    depth = pre_g.shape[0]
    h3 = x
    for l in range(depth):
        lam_init = 0.8 - 0.6 * math.exp(-0.3 * l)
        wg = _block_diag_gate_weights(lru_wa[l], lru_wx[l]).astype(BF16)
        bg = jnp.concatenate([lru_ba[l].reshape(RNN_TILES, 1, RNN_CB),
                              lru_bx[l].reshape(RNN_TILES, 1, RNN_CB)], axis=-1)
        z3, y_r = _in_proj_rnn(h3, pre_g[l][None, :],
                               (w_in[l] * _q_column_scale()).astype(BF16),
                               conv_w[l], conv_b[l][None, :], wg, bg, lru_a[l][None, :])

        y_a = _attn_branch(z3, attn_lq1[l][None, :], attn_lk1[l][None, :],
                           attn_lq2[l][None, :], attn_lk2[l][None, :],
                           subln_g[l][None, :], lam_init)

        h2 = _merge(h3.reshape(n_b * seq, d), y_r.reshape(n_b * seq, d),
                    y_a.reshape(n_b * seq, d), z3.reshape(n_b * seq, Z_COLS),
                    w_br_rnn[l].astype(BF16), w_br_attn[l].astype(BF16),
                    w_out[l].astype(BF16), post_g[l][None, :])
        h3 = h2.reshape(n_b, seq, d)
    return h3
```

```python
import functools
import math

import jax
import jax.numpy as jnp
import numpy as np
from jax import lax
from jax.experimental import pallas as pl
from jax.experimental.pallas import tpu as pltpu

F32 = jnp.float32
BF16 = jnp.bfloat16

D_MODEL = 1024
RNN_BLOCKS = 16
RNN_BLOCK = 64
CONV_WIDTH = 4
LRU_C = 8.0
N_HEADS = 8
HEAD_DIM = 64
V_DIM = 128
NORM_EPS = 1e-6
D_IN_TOTAL = 8192

LANES = 128
COL_XR, COL_ZR, COL_Q, COL_K, COL_V, COL_ZA, COL_GM = 0, 8, 16, 24, 32, 40, 48
Z_COL0 = COL_Q

VMEM_LIMIT = 56 * 1024 * 1024


LOG2E = math.log2(math.e)


def _sigmoid(x):
    return 1.0 / (1.0 + jnp.exp2(x * (-LOG2E)))


RNN_CB = 256
RNN_TILES = D_MODEL // RNN_CB
RNN_SLABS = D_MODEL // LANES
INRNN_TB = 64
INPROJ_TN = 512
INRNN_LEAD = 2
SQRT_FLOOR = 1e-30
Z_COLS = D_IN_TOTAL - 2 * D_MODEL


def _inproj_rnn_kernel(x_ref, g_ref, w_ref, cw_ref, cb_ref, wg_ref, bg_ref, la_ref,
                       z_ref, yr_ref, xt_ref, zr_ref, a_ref, b_ref, h_ref):
    n_b, tb = x_ref.shape[0], x_ref.shape[1]
    rows = n_b * tb
    hdr = (CONV_WIDTH - 1) * n_b

    @pl.when(pl.program_id(0) == 0)
    def _():
        xt_ref[:, 0:hdr, :] = jnp.zeros((RNN_SLABS, hdr, LANES), F32)
        h_ref[...] = jnp.zeros_like(h_ref)

    x = x_ref[...].reshape(rows, D_MODEL)
    ms = jnp.mean(x * x, axis=-1, keepdims=True)
    u = (x * lax.rsqrt(ms + NORM_EPS) * g_ref[...]).astype(BF16)

    xr = jnp.dot(u, w_ref[:, 0:D_MODEL], preferred_element_type=F32)
    for b in range(n_b):
        for s in range(RNN_SLABS):
            xt_ref[s, pl.ds(hdr + b, tb, stride=n_b), :] = (
                xr[b * tb:(b + 1) * tb, s * LANES:(s + 1) * LANES])
    zr_ref[...] = jnp.dot(u, w_ref[:, D_MODEL:2 * D_MODEL], preferred_element_type=F32)

    def project(j):
        cols = slice(j * INPROJ_TN, (j + 1) * INPROJ_TN)
        wcols = slice(2 * D_MODEL + j * INPROJ_TN, 2 * D_MODEL + (j + 1) * INPROJ_TN)
        z_ref[:, :, cols] = jnp.dot(u, w_ref[:, wcols], preferred_element_type=F32
                                    ).astype(BF16).reshape(n_b, tb, INPROJ_TN)

    n_proj = Z_COLS // INPROJ_TN
    for j in range(INRNN_LEAD):
        project(j)

    la = la_ref[...]
    log2a_scale = (-LRU_C * LOG2E) * (jnp.maximum(-la, 0.0)
                                      + jnp.log1p(jnp.exp(-jnp.abs(la))))
    per = RNN_CB // LANES
    for ct in range(RNN_TILES):
        ch = slice(ct * RNN_CB, (ct + 1) * RNN_CB)
        xc = cb_ref[:, ch]
        for tap in range(CONV_WIDTH):
            off = hdr - (CONV_WIDTH - 1 - tap) * n_b
            x_tap = jnp.concatenate(
                [xt_ref[ct * per + s, off:off + rows, :] for s in range(per)], axis=1)
            xc = xc + cw_ref[tap:tap + 1, ch] * x_tap
        xcb = xc.astype(BF16)
        sg = []
        for gate in range(2):
            gcols = slice(gate * RNN_CB, (gate + 1) * RNN_CB)
            pre = jnp.dot(xcb, wg_ref[ct, :, gcols], preferred_element_type=F32)
            sg.append(_sigmoid(pre + bg_ref[ct, :, gcols]))
            project(INRNN_LEAD + ct * 2 + gate)
        a = jnp.exp2(log2a_scale[:, ch] * sg[0])
        y = 1.0 - a * a
        bb = (y * lax.rsqrt(jnp.maximum(y, SQRT_FLOOR))) * (sg[1] * xc)
        for s in range(per):
            a_ref[ct * per + s] = a[:, s * LANES:(s + 1) * LANES]
            b_ref[ct * per + s] = bb[:, s * LANES:(s + 1) * LANES]

    for s in range(RNN_SLABS):
        xt_ref[s, 0:hdr, :] = xt_ref[s, rows:rows + hdr, :]

    for j in range(INRNN_LEAD + RNN_TILES * 2, n_proj):
        project(j)

    hs = [h_ref[s] for s in range(RNN_SLABS)]
    for t in range(tb):
        for s in range(RNN_SLABS):
            hs[s] = a_ref[s, t * n_b:(t + 1) * n_b, :] * hs[s] + b_ref[s, t * n_b:(t + 1) * n_b, :]
            b_ref[s, t * n_b:(t + 1) * n_b, :] = hs[s]
    for s in range(RNN_SLABS):
        h_ref[s] = hs[s]

    for b in range(n_b):
        h = jnp.concatenate(
            [b_ref[s, pl.ds(b, tb, stride=n_b), :] for s in range(RNN_SLABS)], axis=1)
        zr = zr_ref[b * tb:(b + 1) * tb, :]
        yr_ref[b] = (h * (zr * _sigmoid(zr))).astype(BF16)


def _in_proj_rnn(x3, g, w, conv_w, conv_b, wg, bg, lru_a):
    n_b, seq, _ = x3.shape
    rows = n_b * INRNN_TB
    const = lambda *shape: pl.BlockSpec(shape, lambda i: (0,) * len(shape))
    return pl.pallas_call(
        _inproj_rnn_kernel,
        grid=(seq // INRNN_TB,),
        in_specs=[
            pl.BlockSpec((n_b, INRNN_TB, D_MODEL), lambda i: (0, i, 0)),
            const(1, D_MODEL),
            pl.BlockSpec((D_MODEL, D_IN_TOTAL), lambda i: (0, 0), pipeline_mode=pl.Buffered(1)),
            const(CONV_WIDTH, D_MODEL), const(1, D_MODEL),
            const(RNN_TILES, RNN_CB, 2 * RNN_CB), const(RNN_TILES, 1, 2 * RNN_CB),
            const(1, D_MODEL),
        ],
        out_specs=[
            pl.BlockSpec((n_b, INRNN_TB, Z_COLS), lambda i: (0, i, 0)),
            pl.BlockSpec((n_b, INRNN_TB, D_MODEL), lambda i: (0, i, 0)),
        ],
        out_shape=[
            jax.ShapeDtypeStruct((n_b, seq, Z_COLS), BF16),
            jax.ShapeDtypeStruct((n_b, seq, D_MODEL), BF16),
        ],
        scratch_shapes=[
            pltpu.VMEM((RNN_SLABS, (CONV_WIDTH - 1) * n_b + rows, LANES), F32),
            pltpu.VMEM((rows, D_MODEL), F32),
            pltpu.VMEM((RNN_SLABS, rows, LANES), F32),
            pltpu.VMEM((RNN_SLABS, rows, LANES), F32),
            pltpu.VMEM((RNN_SLABS, n_b, LANES), F32),
        ],
        compiler_params=pltpu.CompilerParams(
            dimension_semantics=("arbitrary",),
            vmem_limit_bytes=VMEM_LIMIT),
        name="in_proj_rnn",
    )(x3, g, w, conv_w, conv_b, wg, bg, lru_a)


ATT_TQ = 512
ATT_KC = 512
ATT_NC = 1
BF16_ROWS_PER_VREG = 16
ATT_VROWS = V_DIM + BF16_ROWS_PER_VREG
ATT_BOUND = 32.0


def _attn_kernel(lq1_ref, lk1_ref, lq2_ref, lk2_ref, sg_ref, q_ref, k_ref, v_ref, za_ref,
                 o_ref, q2_ref, vt_ref, bias_ref, s_ref, m_ref, acc_ref, *, lam_init):
    tq, kc, nc = ATT_TQ, ATT_KC, ATT_NC
    assert tq == nc * kc
    n_tiles = q_ref.shape[1] // tq
    n_chunks = v_ref.shape[1] // kc

    lam = (jnp.exp(jnp.sum(lq1_ref[...] * lk1_ref[...], axis=-1, keepdims=True))
           - jnp.exp(jnp.sum(lq2_ref[...] * lk2_ref[...], axis=-1, keepdims=True))
           + lam_init)

    @pl.when((pl.program_id(0) == 0) & (pl.program_id(1) == 0))
    def _():
        row = lax.broadcasted_iota(jnp.int32, (kc, 2 * tq), 0)
        col = lax.broadcasted_iota(jnp.int32, (kc, 2 * tq), 1)
        diff = row - jnp.where(col >= tq, col - tq, col)
        for u in range(nc):
            bias_ref[u] = jnp.where(diff <= -u * kc, 0.0, -jnp.inf)

    dim = lax.broadcasted_iota(jnp.int32, (2 * HEAD_DIM, tq), 0)
    qn2 = None
    for t in range(n_tiles):
        qt = q_ref[0, t * tq:(t + 1) * tq, :].T
        zero = jnp.zeros_like(qt)
        norms = []
        for half, keep in enumerate((dim < HEAD_DIM, dim >= HEAD_DIM)):
            qh = jnp.where(keep, qt, zero)
            q2_ref[t, :, half * tq:(half + 1) * tq] = qh
            qf = qh.astype(F32)
            norms.append(jnp.sum(qf * qf, axis=0, keepdims=True))
        n2 = jnp.concatenate(norms, axis=1)
        qn2 = n2 if qn2 is None else jnp.maximum(qn2, n2)

    ones_rows = jnp.where(
        lax.broadcasted_iota(jnp.int32, (ATT_VROWS - V_DIM, kc), 0) == 0, 1.0, 0.0
    ).astype(BF16)
    for c in range(n_chunks):
        vt_ref[c, 0:V_DIM, :] = v_ref[0, c * kc:(c + 1) * kc, :].T
        vt_ref[c, V_DIM:ATT_VROWS, :] = ones_rows

    klane = lax.broadcasted_iota(jnp.int32, (kc, 2 * HEAD_DIM), 1)
    kn2 = [None, None]
    for c in range(n_chunks):
        kf = k_ref[0, c * kc:(c + 1) * kc, :].astype(F32)
        sq = kf * kf
        for half, keep in enumerate((klane < HEAD_DIM, klane >= HEAD_DIM)):
            n2 = jnp.max(jnp.sum(jnp.where(keep, sq, 0.0), axis=1, keepdims=True),
                         axis=0, keepdims=True)
            kn2[half] = n2 if kn2[half] is None else jnp.maximum(kn2[half], n2)
    kcol = lax.broadcasted_iota(jnp.int32, (1, 2 * tq), 1)
    bound2 = qn2 * jnp.where(kcol < tq, kn2[0], kn2[1])
    bounded = jnp.max(bound2) <= ATT_BOUND * ATT_BOUND

    def scores(t, c):
        if isinstance(c, int):
            kt = k_ref[0, c * kc:(c + 1) * kc, :]
        else:
            kt = k_ref[0, pl.ds(pl.multiple_of(c * kc, kc), kc), :]
        return jnp.dot(kt, q2_ref[t], preferred_element_type=F32)

    def finalize(t, acc):
        o2 = acc[0:V_DIM] * (1.0 / acc[V_DIM:V_DIM + 1])
        o = (o2[:, :tq] - lam * o2[:, tq:]).T
        o = o * lax.rsqrt(jnp.mean(o * o, axis=-1, keepdims=True) + NORM_EPS)
        o = o * sg_ref[...] * (1.0 - lam_init)
        start = t * tq if isinstance(t, int) else pl.multiple_of(t * tq, tq)
        za = za_ref[0, pl.ds(start, tq), :].astype(F32)
        o_ref[0, pl.ds(start, tq), :] = (o * (za * _sigmoid(za))).astype(BF16)

    for t in range(n_tiles):
        acc = None
        for c in range((t + 1) * nc):
            s = scores(t, c)
            if c >= t * nc:
                s = s + bias_ref[c - t * nc]
            pv = jnp.dot(vt_ref[c], jnp.exp2(s).astype(BF16),
                         preferred_element_type=F32)
            acc = pv if acc is None else acc + pv
        finalize(t, acc)

    @pl.when(jnp.logical_not(bounded))
    def _():
        def step(t, pair, first):
            for u in range(nc):
                s = scores(t, pair * nc + u)
                s_ref[u] = s + bias_ref[u] if first else s
            m_blk = functools.reduce(
                jnp.maximum, [jnp.max(s_ref[u], axis=0, keepdims=True) for u in range(nc)])
            if first:
                m_new = m_blk
            else:
                m_new = jnp.maximum(m_ref[...], m_blk)
                alpha = jnp.exp2(m_ref[...] - m_new)
            m_ref[...] = m_new
            pv = None
            for u in range(nc):
                d = jnp.dot(vt_ref[pair * nc + u], jnp.exp2(s_ref[u] - m_new).astype(BF16),
                            preferred_element_type=F32)
                pv = d if pv is None else pv + d
            acc_ref[...] = pv if first else alpha * acc_ref[...] + pv

        def tile_body(t, _):
            step(t, t, True)

            def rest(pair, _):
                step(t, pair, False)
                return 0
            lax.fori_loop(0, t, rest, 0)
            finalize(t, acc_ref[...])
            return 0
        lax.fori_loop(0, n_tiles, tile_body, 0)


def _attn_branch(z3, lq1, lk1, lq2, lk2, subln_g, lam_init):
    n_b, seq, _ = z3.shape
    small = lambda n: pl.BlockSpec((1, n), lambda b, h: (0, 0))
    head = lambda col: pl.BlockSpec((1, seq, LANES), lambda b, h: (b, 0, col - Z_COL0 + h))
    return pl.pallas_call(
        functools.partial(_attn_kernel, lam_init=lam_init),
        grid=(n_b, N_HEADS),
        in_specs=[
            small(HEAD_DIM), small(HEAD_DIM), small(HEAD_DIM), small(HEAD_DIM),
            small(V_DIM),
            head(COL_Q), head(COL_K), head(COL_V), head(COL_ZA),
        ],
        out_specs=pl.BlockSpec((1, seq, LANES), lambda b, h: (b, 0, h)),
        out_shape=jax.ShapeDtypeStruct((n_b, seq, N_HEADS * V_DIM), BF16),
        scratch_shapes=[
            pltpu.VMEM((seq // ATT_TQ, LANES, 2 * ATT_TQ), BF16),
            pltpu.VMEM((seq // ATT_KC, ATT_VROWS, ATT_KC), BF16),
            pltpu.VMEM((ATT_NC, ATT_KC, 2 * ATT_TQ), F32),
            pltpu.VMEM((ATT_NC, ATT_KC, 2 * ATT_TQ), F32),
            pltpu.VMEM((1, 2 * ATT_TQ), F32),
            pltpu.VMEM((ATT_VROWS, 2 * ATT_TQ), F32),
        ],
        compiler_params=pltpu.CompilerParams(
            dimension_semantics=("arbitrary", "arbitrary"),
            vmem_limit_bytes=VMEM_LIMIT),
        name="diff_attn",
    )(lq1, lk1, lq2, lk2, subln_g, z3, z3, z3, z3)


def _merge_kernel(x_ref, yr_ref, ya_ref, gm_ref, wr_ref, wa_ref, wo_ref, pg_ref, o_ref):
    pr = jnp.dot(yr_ref[...], wr_ref[...], preferred_element_type=F32)
    pa = jnp.dot(ya_ref[...], wa_ref[...], preferred_element_type=F32)
    g = _sigmoid(gm_ref[...].astype(F32))
    m = g[:, :D_MODEL] * pr + g[:, D_MODEL:] * pa
    y = jnp.dot(m.astype(BF16), wo_ref[...], preferred_element_type=F32)
    y = y * lax.rsqrt(jnp.mean(y * y, axis=-1, keepdims=True) + NORM_EPS)
    o_ref[...] = x_ref[...] + y * pg_ref[...]


def _merge(x2, yr, ya, z2, wr, wa, wo, post_g, tm=1024):
    n_tok = x2.shape[0]
    full = lambda r, c: pl.BlockSpec((r, c), lambda i: (0, 0), pipeline_mode=pl.Buffered(1))
    return pl.pallas_call(
        _merge_kernel,
        grid=(n_tok // tm,),
        in_specs=[
            pl.BlockSpec((tm, D_MODEL), lambda i: (i, 0)),
            pl.BlockSpec((tm, D_MODEL), lambda i: (i, 0)),
            pl.BlockSpec((tm, D_MODEL), lambda i: (i, 0)),
            pl.BlockSpec((tm, 2 * D_MODEL),
                         lambda i: (i, (COL_GM - Z_COL0) * LANES // (2 * D_MODEL))),
            full(D_MODEL, D_MODEL), full(D_MODEL, D_MODEL), full(D_MODEL, D_MODEL),
            full(1, D_MODEL),
        ],
        out_specs=pl.BlockSpec((tm, D_MODEL), lambda i: (i, 0)),
        out_shape=jax.ShapeDtypeStruct((n_tok, D_MODEL), F32),
        compiler_params=pltpu.CompilerParams(
            dimension_semantics=("arbitrary",),
            vmem_limit_bytes=VMEM_LIMIT),
        name="merge",
    )(x2, yr, ya, z2, wr, wa, wo, post_g)


def _q_column_scale():
    col = np.ones((1, D_IN_TOTAL), np.float32)
    col[:, COL_Q * LANES:COL_K * LANES] = HEAD_DIM ** -0.5 * math.log2(math.e)
    return jnp.asarray(col)


def _block_diag_gate_weights(wa, wx):
    per = RNN_CB // RNN_BLOCK

    def expand(w):
        w = w.reshape(D_MODEL // RNN_CB, per, RNN_BLOCK, RNN_BLOCK)
        eye = jnp.eye(per, dtype=w.dtype)
        return jnp.einsum('cgij,gh->cgihj', w, eye).reshape(
            D_MODEL // RNN_CB, RNN_CB, RNN_CB)

    return jnp.concatenate([expand(wa), expand(wx)], axis=-1)


def kernel(x, pre_g, post_g, w_in, conv_w, conv_b, lru_wa, lru_ba, lru_wx, lru_bx, lru_a,
           attn_lq1, attn_lk1, attn_lq2, attn_lk2, subln_g, w_br_rnn, w_br_attn, w_out):
    n_b, seq, d = x.shape
    depth = pre_g.shape[0]
    h3 = x
    for l in range(depth):
        lam_init = 0.8 - 0.6 * math.exp(-0.3 * l)
        wg = _block_diag_gate_weights(lru_wa[l], lru_wx[l]).astype(BF16)
        bg = jnp.concatenate([lru_ba[l].reshape(RNN_TILES, 1, RNN_CB),
                              lru_bx[l].reshape(RNN_TILES, 1, RNN_CB)], axis=-1)
        z3, y_r = _in_proj_rnn(h3, pre_g[l][None, :],
                               (w_in[l] * _q_column_scale()).astype(BF16),
                               conv_w[l], conv_b[l][None, :], wg, bg, lru_a[l][None, :])

        y_a = _attn_branch(z3, attn_lq1[l][None, :], attn_lk1[l][None, :],
                           attn_lq2[l][None, :], attn_lk2[l][None, :],
                           subln_g[l][None, :], lam_init)

        h2 = _merge(h3.reshape(n_b * seq, d), y_r.reshape(n_b * seq, d),
                    y_a.reshape(n_b * seq, d), z3.reshape(n_b * seq, Z_COLS),
                    w_br_rnn[l].astype(BF16), w_br_attn[l].astype(BF16),
                    w_out[l].astype(BF16), post_g[l][None, :])
        h3 = h2.reshape(n_b, seq, d)
    return h3
```

```python
import functools
import math

import jax
import jax.numpy as jnp
import numpy as np
from jax import lax
from jax.experimental import pallas as pl
from jax.experimental.pallas import tpu as pltpu

F32 = jnp.float32
BF16 = jnp.bfloat16

D_MODEL = 1024
RNN_BLOCKS = 16
RNN_BLOCK = 64
CONV_WIDTH = 4
LRU_C = 8.0
N_HEADS = 8
HEAD_DIM = 64
V_DIM = 128
NORM_EPS = 1e-6
D_IN_TOTAL = 8192

LANES = 128
COL_XR, COL_ZR, COL_Q, COL_K, COL_V, COL_ZA, COL_GM = 0, 8, 16, 24, 32, 40, 48
Z_COL0 = COL_Q

VMEM_LIMIT = 56 * 1024 * 1024


LOG2E = math.log2(math.e)


def _sigmoid(x):
    return 1.0 / (1.0 + jnp.exp2(x * (-LOG2E)))


RNN_CB = 256
RNN_TILES = D_MODEL // RNN_CB
RNN_SLABS = D_MODEL // LANES
INRNN_TB = 64
INPROJ_TN = 512
INRNN_LEAD = 2
SQRT_FLOOR = 1e-30
Z_COLS = D_IN_TOTAL - 2 * D_MODEL


def _inproj_rnn_kernel(x_ref, g_ref, w_ref, cw_ref, cb_ref, wg_ref, bg_ref, la_ref,
                       z_ref, yr_ref, xt_ref, zr_ref, a_ref, b_ref, h_ref):
    n_b, tb = x_ref.shape[0], x_ref.shape[1]
    rows = n_b * tb
    hdr = (CONV_WIDTH - 1) * n_b

    @pl.when(pl.program_id(0) == 0)
    def _():
        xt_ref[:, 0:hdr, :] = jnp.zeros((RNN_SLABS, hdr, LANES), F32)
        h_ref[...] = jnp.zeros_like(h_ref)

    x = x_ref[...].reshape(rows, D_MODEL)
    ms = jnp.mean(x * x, axis=-1, keepdims=True)
    u = (x * lax.rsqrt(ms + NORM_EPS) * g_ref[...]).astype(BF16)

    xr = jnp.dot(u, w_ref[:, 0:D_MODEL], preferred_element_type=F32)
    for b in range(n_b):
        for s in range(RNN_SLABS):
            xt_ref[s, pl.ds(hdr + b, tb, stride=n_b), :] = (
                xr[b * tb:(b + 1) * tb, s * LANES:(s + 1) * LANES])
    zr_ref[...] = jnp.dot(u, w_ref[:, D_MODEL:2 * D_MODEL], preferred_element_type=F32)

    def project(j):
        cols = slice(j * INPROJ_TN, (j + 1) * INPROJ_TN)
        wcols = slice(2 * D_MODEL + j * INPROJ_TN, 2 * D_MODEL + (j + 1) * INPROJ_TN)
        z_ref[:, :, cols] = jnp.dot(u, w_ref[:, wcols], preferred_element_type=F32
                                    ).astype(BF16).reshape(n_b, tb, INPROJ_TN)

    n_proj = Z_COLS // INPROJ_TN
    for j in range(INRNN_LEAD):
        project(j)

    la = la_ref[...]
    log2a_scale = (-LRU_C * LOG2E) * (jnp.maximum(-la, 0.0)
                                      + jnp.log1p(jnp.exp(-jnp.abs(la))))
    per = RNN_CB // LANES
    for ct in range(RNN_TILES):
        ch = slice(ct * RNN_CB, (ct + 1) * RNN_CB)
        xc = cb_ref[:, ch]
        for tap in range(CONV_WIDTH):
            off = hdr - (CONV_WIDTH - 1 - tap) * n_b
            x_tap = jnp.concatenate(
                [xt_ref[ct * per + s, off:off + rows, :] for s in range(per)], axis=1)
            xc = xc + cw_ref[tap:tap + 1, ch] * x_tap
        xcb = xc.astype(BF16)
        sg = []
        for gate in range(2):
            gcols = slice(gate * RNN_CB, (gate + 1) * RNN_CB)
            pre = jnp.dot(xcb, wg_ref[ct, :, gcols], preferred_element_type=F32)
            sg.append(_sigmoid(pre + bg_ref[ct, :, gcols]))
            project(INRNN_LEAD + ct * 2 + gate)
        a = jnp.exp2(log2a_scale[:, ch] * sg[0])
        y = 1.0 - a * a
        bb = (y * lax.rsqrt(jnp.maximum(y, SQRT_FLOOR))) * (sg[1] * xc)
        for s in range(per):
            a_ref[ct * per + s] = a[:, s * LANES:(s + 1) * LANES]
            b_ref[ct * per + s] = bb[:, s * LANES:(s + 1) * LANES]

    for s in range(RNN_SLABS):
        xt_ref[s, 0:hdr, :] = xt_ref[s, rows:rows + hdr, :]

    for j in range(INRNN_LEAD + RNN_TILES * 2, n_proj):
        project(j)

    hs = [h_ref[s] for s in range(RNN_SLABS)]
    for t in range(tb):
        for s in range(RNN_SLABS):
            hs[s] = a_ref[s, t * n_b:(t + 1) * n_b, :] * hs[s] + b_ref[s, t * n_b:(t + 1) * n_b, :]
            b_ref[s, t * n_b:(t + 1) * n_b, :] = hs[s]
    for s in range(RNN_SLABS):
        h_ref[s] = hs[s]

    for b in range(n_b):
        h = jnp.concatenate(
            [b_ref[s, pl.ds(b, tb, stride=n_b), :] for s in range(RNN_SLABS)], axis=1)
        zr = zr_ref[b * tb:(b + 1) * tb, :]
        yr_ref[b] = (h * (zr * _sigmoid(zr))).astype(BF16)


def _in_proj_rnn(x3, g, w, conv_w, conv_b, wg, bg, lru_a):
    n_b, seq, _ = x3.shape
    rows = n_b * INRNN_TB
    const = lambda *shape: pl.BlockSpec(shape, lambda i: (0,) * len(shape))
    return pl.pallas_call(
        _inproj_rnn_kernel,
        grid=(seq // INRNN_TB,),
        in_specs=[
            pl.BlockSpec((n_b, INRNN_TB, D_MODEL), lambda i: (0, i, 0)),
            const(1, D_MODEL),
            pl.BlockSpec((D_MODEL, D_IN_TOTAL), lambda i: (0, 0), pipeline_mode=pl.Buffered(1)),
            const(CONV_WIDTH, D_MODEL), const(1, D_MODEL),
            const(RNN_TILES, RNN_CB, 2 * RNN_CB), const(RNN_TILES, 1, 2 * RNN_CB),
            const(1, D_MODEL),
        ],
        out_specs=[
            pl.BlockSpec((n_b, INRNN_TB, Z_COLS), lambda i: (0, i, 0)),
            pl.BlockSpec((n_b, INRNN_TB, D_MODEL), lambda i: (0, i, 0)),
        ],
        out_shape=[
            jax.ShapeDtypeStruct((n_b, seq, Z_COLS), BF16),
            jax.ShapeDtypeStruct((n_b, seq, D_MODEL), BF16),
        ],
        scratch_shapes=[
            pltpu.VMEM((RNN_SLABS, (CONV_WIDTH - 1) * n_b + rows, LANES), F32),
            pltpu.VMEM((rows, D_MODEL), F32),
            pltpu.VMEM((RNN_SLABS, rows, LANES), F32),
            pltpu.VMEM((RNN_SLABS, rows, LANES), F32),
            pltpu.VMEM((RNN_SLABS, n_b, LANES), F32),
        ],
        compiler_params=pltpu.CompilerParams(
            dimension_semantics=("arbitrary",),
            vmem_limit_bytes=VMEM_LIMIT),
        name="in_proj_rnn",
    )(x3, g, w, conv_w, conv_b, wg, bg, lru_a)


ATT_TQ = 512
ATT_KC = 512
ATT_NC = 1
BF16_ROWS_PER_VREG = 16
ATT_VROWS = V_DIM + BF16_ROWS_PER_VREG
ATT_BOUND = 32.0


def _attn_kernel(lq1_ref, lk1_ref, lq2_ref, lk2_ref, sg_ref, q_ref, k_ref, v_ref, za_ref,
                 o_ref, q2_ref, vt_ref, bias_ref, s_ref, m_ref, acc_ref, *, lam_init):
    tq, kc, nc = ATT_TQ, ATT_KC, ATT_NC
    assert tq == nc * kc
    n_tiles = q_ref.shape[1] // tq
    n_chunks = v_ref.shape[1] // kc

    lam = (jnp.exp(jnp.sum(lq1_ref[...] * lk1_ref[...], axis=-1, keepdims=True))
           - jnp.exp(jnp.sum(lq2_ref[...] * lk2_ref[...], axis=-1, keepdims=True))
           + lam_init)

    @pl.when((pl.program_id(0) == 0) & (pl.program_id(1) == 0))
    def _():
        row = lax.broadcasted_iota(jnp.int32, (kc, 2 * tq), 0)
        col = lax.broadcasted_iota(jnp.int32, (kc, 2 * tq), 1)
        diff = row - jnp.where(col >= tq, col - tq, col)
        for u in range(nc):
            bias_ref[u] = jnp.where(diff <= -u * kc, 0.0, -jnp.inf)

    def max_norms2(x, lane, acc):
        sq = x.astype(F32)
        sq = sq * sq
        for half, keep in enumerate((lane < HEAD_DIM, lane >= HEAD_DIM)):
            n2 = jnp.max(jnp.sum(jnp.where(keep, sq, 0.0), axis=1, keepdims=True),
                         axis=0, keepdims=True)
            acc[half] = n2 if acc[half] is None else jnp.maximum(acc[half], n2)

    qlane = lax.broadcasted_iota(jnp.int32, (tq, 2 * HEAD_DIM), 1)
    qn2 = [None, None]
    for t in range(n_tiles):
        q = q_ref[0, t * tq:(t + 1) * tq, :]
        zero = jnp.zeros_like(q)
        q2_ref[t, 0:tq, :] = jnp.where(qlane < HEAD_DIM, q, zero)
        q2_ref[t, tq:2 * tq, :] = jnp.where(qlane >= HEAD_DIM, q, zero)
        max_norms2(q, qlane, qn2)

    ones_rows = jnp.where(
        lax.broadcasted_iota(jnp.int32, (ATT_VROWS - V_DIM, kc), 0) == 0, 1.0, 0.0
    ).astype(BF16)
    for c in range(n_chunks):
        vt_ref[c, 0:V_DIM, :] = v_ref[0, c * kc:(c + 1) * kc, :].T
        vt_ref[c, V_DIM:ATT_VROWS, :] = ones_rows

    klane = lax.broadcasted_iota(jnp.int32, (kc, 2 * HEAD_DIM), 1)
    kn2 = [None, None]
    for c in range(n_chunks):
        max_norms2(k_ref[0, c * kc:(c + 1) * kc, :], klane, kn2)
    bound2 = jnp.maximum(qn2[0] * kn2[0], qn2[1] * kn2[1])
    bounded = jnp.max(bound2) <= ATT_BOUND * ATT_BOUND

    def scores(t, c):
        if isinstance(c, int):
            kt = k_ref[0, c * kc:(c + 1) * kc, :]
        else:
            kt = k_ref[0, pl.ds(pl.multiple_of(c * kc, kc), kc), :]
        return lax.dot_general(kt, q2_ref[t], (((1,), (1,)), ((), ())),
                               preferred_element_type=F32)

    def finalize(t, acc):
        o2 = acc[0:V_DIM] * (1.0 / acc[V_DIM:V_DIM + 1])
        o = (o2[:, :tq] - lam * o2[:, tq:]).T
        o = o * lax.rsqrt(jnp.mean(o * o, axis=-1, keepdims=True) + NORM_EPS)
        o = o * sg_ref[...] * (1.0 - lam_init)
        start = t * tq if isinstance(t, int) else pl.multiple_of(t * tq, tq)
        za = za_ref[0, pl.ds(start, tq), :].astype(F32)
        o_ref[0, pl.ds(start, tq), :] = (o * (za * _sigmoid(za))).astype(BF16)

    for t in range(n_tiles):
        acc = None
        for c in range((t + 1) * nc):
            s = scores(t, c)
            if c >= t * nc:
                s = s + bias_ref[c - t * nc]
            pv = jnp.dot(vt_ref[c], jnp.exp2(s).astype(BF16),
                         preferred_element_type=F32)
            acc = pv if acc is None else acc + pv
        finalize(t, acc)

    @pl.when(jnp.logical_not(bounded))
    def _():
        def step(t, pair, first):
            for u in range(nc):
                s = scores(t, pair * nc + u)
                s_ref[u] = s + bias_ref[u] if first else s
            m_blk = functools.reduce(
                jnp.maximum, [jnp.max(s_ref[u], axis=0, keepdims=True) for u in range(nc)])
            if first:
                m_new = m_blk
            else:
                m_new = jnp.maximum(m_ref[...], m_blk)
                alpha = jnp.exp2(m_ref[...] - m_new)
            m_ref[...] = m_new
            pv = None
            for u in range(nc):
                d = jnp.dot(vt_ref[pair * nc + u], jnp.exp2(s_ref[u] - m_new).astype(BF16),
                            preferred_element_type=F32)
                pv = d if pv is None else pv + d
            acc_ref[...] = pv if first else alpha * acc_ref[...] + pv

        def tile_body(t, _):
            step(t, t, True)

            def rest(pair, _):
                step(t, pair, False)
                return 0
            lax.fori_loop(0, t, rest, 0)
            finalize(t, acc_ref[...])
            return 0
        lax.fori_loop(0, n_tiles, tile_body, 0)


def _attn_branch(z3, lq1, lk1, lq2, lk2, subln_g, lam_init):
    n_b, seq, _ = z3.shape
    small = lambda n: pl.BlockSpec((1, n), lambda b, h: (0, 0))
    head = lambda col: pl.BlockSpec((1, seq, LANES), lambda b, h: (b, 0, col - Z_COL0 + h))
    return pl.pallas_call(
        functools.partial(_attn_kernel, lam_init=lam_init),
        grid=(n_b, N_HEADS),
        in_specs=[
            small(HEAD_DIM), small(HEAD_DIM), small(HEAD_DIM), small(HEAD_DIM),
            small(V_DIM),
            head(COL_Q), head(COL_K), head(COL_V), head(COL_ZA),
        ],
        out_specs=pl.BlockSpec((1, seq, LANES), lambda b, h: (b, 0, h)),
        out_shape=jax.ShapeDtypeStruct((n_b, seq, N_HEADS * V_DIM), BF16),
        scratch_shapes=[
            pltpu.VMEM((seq // ATT_TQ, 2 * ATT_TQ, LANES), BF16),
            pltpu.VMEM((seq // ATT_KC, ATT_VROWS, ATT_KC), BF16),
            pltpu.VMEM((ATT_NC, ATT_KC, 2 * ATT_TQ), F32),
            pltpu.VMEM((ATT_NC, ATT_KC, 2 * ATT_TQ), F32),
            pltpu.VMEM((1, 2 * ATT_TQ), F32),
            pltpu.VMEM((ATT_VROWS, 2 * ATT_TQ), F32),
        ],
        compiler_params=pltpu.CompilerParams(
            dimension_semantics=("arbitrary", "arbitrary"),
            vmem_limit_bytes=VMEM_LIMIT),
        name="diff_attn",
    )(lq1, lk1, lq2, lk2, subln_g, z3, z3, z3, z3)


def _merge_kernel(x_ref, yr_ref, ya_ref, gm_ref, wr_ref, wa_ref, wo_ref, pg_ref, o_ref):
    pr = jnp.dot(yr_ref[...], wr_ref[...], preferred_element_type=F32)
    pa = jnp.dot(ya_ref[...], wa_ref[...], preferred_element_type=F32)
    g = _sigmoid(gm_ref[...].astype(F32))
    m = g[:, :D_MODEL] * pr + g[:, D_MODEL:] * pa
    y = jnp.dot(m.astype(BF16), wo_ref[...], preferred_element_type=F32)
    y = y * lax.rsqrt(jnp.mean(y * y, axis=-1, keepdims=True) + NORM_EPS)
    o_ref[...] = x_ref[...] + y * pg_ref[...]


def _merge(x2, yr, ya, z2, wr, wa, wo, post_g, tm=1024):
    n_tok = x2.shape[0]
    full = lambda r, c: pl.BlockSpec((r, c), lambda i: (0, 0), pipeline_mode=pl.Buffered(1))
    return pl.pallas_call(
        _merge_kernel,
        grid=(n_tok // tm,),
        in_specs=[
            pl.BlockSpec((tm, D_MODEL), lambda i: (i, 0)),
            pl.BlockSpec((tm, D_MODEL), lambda i: (i, 0)),
            pl.BlockSpec((tm, D_MODEL), lambda i: (i, 0)),
            pl.BlockSpec((tm, 2 * D_MODEL),
                         lambda i: (i, (COL_GM - Z_COL0) * LANES // (2 * D_MODEL))),
            full(D_MODEL, D_MODEL), full(D_MODEL, D_MODEL), full(D_MODEL, D_MODEL),
            full(1, D_MODEL),
        ],
        out_specs=pl.BlockSpec((tm, D_MODEL), lambda i: (i, 0)),
        out_shape=jax.ShapeDtypeStruct((n_tok, D_MODEL), F32),
        compiler_params=pltpu.CompilerParams(
            dimension_semantics=("arbitrary",),
            vmem_limit_bytes=VMEM_LIMIT),
        name="merge",
    )(x2, yr, ya, z2, wr, wa, wo, post_g)


def _q_column_scale():
    col = np.ones((1, D_IN_TOTAL), np.float32)
    col[:, COL_Q * LANES:COL_K * LANES] = HEAD_DIM ** -0.5 * math.log2(math.e)
    return jnp.asarray(col)


def _block_diag_gate_weights(wa, wx):
    per = RNN_CB // RNN_BLOCK

    def expand(w):
        w = w.reshape(D_MODEL // RNN_CB, per, RNN_BLOCK, RNN_BLOCK)
        eye = jnp.eye(per, dtype=w.dtype)
        return jnp.einsum('cgij,gh->cgihj', w, eye).reshape(
            D_MODEL // RNN_CB, RNN_CB, RNN_CB)

    return jnp.concatenate([expand(wa), expand(wx)], axis=-1)


def kernel(x, pre_g, post_g, w_in, conv_w, conv_b, lru_wa, lru_ba, lru_wx, lru_bx, lru_a,
           attn_lq1, attn_lk1, attn_lq2, attn_lk2, subln_g, w_br_rnn, w_br_attn, w_out):
    n_b, seq, d = x.shape
    depth = pre_g.shape[0]
    h3 = x
    for l in range(depth):
        lam_init = 0.8 - 0.6 * math.exp(-0.3 * l)
        wg = _block_diag_gate_weights(lru_wa[l], lru_wx[l]).astype(BF16)
        bg = jnp.concatenate([lru_ba[l].reshape(RNN_TILES, 1, RNN_CB),
                              lru_bx[l].reshape(RNN_TILES, 1, RNN_CB)], axis=-1)
        z3, y_r = _in_proj_rnn(h3, pre_g[l][None, :],
                               (w_in[l] * _q_column_scale()).astype(BF16),
                               conv_w[l], conv_b[l][None, :], wg, bg, lru_a[l][None, :])

        y_a = _attn_branch(z3, attn_lq1[l][None, :], attn_lk1[l][None, :],
                           attn_lq2[l][None, :], attn_lk2[l][None, :],
                           subln_g[l][None, :], lam_init)

        h2 = _merge(h3.reshape(n_b * seq, d), y_r.reshape(n_b * seq, d),
                    y_a.reshape(n_b * seq, d), z3.reshape(n_b * seq, Z_COLS),
                    w_br_rnn[l].astype(BF16), w_br_attn[l].astype(BF16),
                    w_out[l].astype(BF16), post_g[l][None, :])
        h3 = h2.reshape(n_b, seq, d)
    return h3
```

```python
import functools
import math

import jax
import jax.numpy as jnp
import numpy as np
from jax import lax
from jax.experimental import pallas as pl
from jax.experimental.pallas import tpu as pltpu

F32 = jnp.float32
BF16 = jnp.bfloat16

D_MODEL = 1024
RNN_BLOCKS = 16
RNN_BLOCK = 64
CONV_WIDTH = 4
LRU_C = 8.0
N_HEADS = 8
HEAD_DIM = 64
V_DIM = 128
NORM_EPS = 1e-6
D_IN_TOTAL = 8192

LANES = 128
COL_XR, COL_ZR, COL_Q, COL_K, COL_V, COL_ZA, COL_GM = 0, 8, 16, 24, 32, 40, 48
Z_COL0 = COL_Q

VMEM_LIMIT = 56 * 1024 * 1024


LOG2E = math.log2(math.e)


def _sigmoid(x):
    return 1.0 / (1.0 + jnp.exp2(x * (-LOG2E)))


RNN_CB = 256
RNN_TILES = D_MODEL // RNN_CB
RNN_SLABS = D_MODEL // LANES
INRNN_TB = 64
INPROJ_TN = 512
INRNN_LEAD = 2
SQRT_FLOOR = 1e-30
Z_COLS = D_IN_TOTAL - 2 * D_MODEL


def _inproj_rnn_kernel(x_ref, g_ref, w_ref, cw_ref, cb_ref, wg_ref, bg_ref, la_ref,
                       z_ref, yr_ref, xt_ref, zr_ref, a_ref, b_ref, h_ref):
    n_b, tb = x_ref.shape[0], x_ref.shape[1]
    rows = n_b * tb
    hdr = (CONV_WIDTH - 1) * n_b

    @pl.when(pl.program_id(0) == 0)
    def _():
        xt_ref[:, 0:hdr, :] = jnp.zeros((RNN_SLABS, hdr, LANES), F32)
        h_ref[...] = jnp.zeros_like(h_ref)

    x = x_ref[...].reshape(rows, D_MODEL)
    ms = jnp.mean(x * x, axis=-1, keepdims=True)
    u = (x * lax.rsqrt(ms + NORM_EPS) * g_ref[...]).astype(BF16)

    xr = jnp.dot(u, w_ref[:, 0:D_MODEL], preferred_element_type=F32)
    for b in range(n_b):
        for s in range(RNN_SLABS):
            xt_ref[s, pl.ds(hdr + b, tb, stride=n_b), :] = (
                xr[b * tb:(b + 1) * tb, s * LANES:(s + 1) * LANES])
    zr_ref[...] = jnp.dot(u, w_ref[:, D_MODEL:2 * D_MODEL], preferred_element_type=F32)

    def project(j):
        cols = slice(j * INPROJ_TN, (j + 1) * INPROJ_TN)
        wcols = slice(2 * D_MODEL + j * INPROJ_TN, 2 * D_MODEL + (j + 1) * INPROJ_TN)
        z_ref[:, :, cols] = jnp.dot(u, w_ref[:, wcols], preferred_element_type=F32
                                    ).astype(BF16).reshape(n_b, tb, INPROJ_TN)

    n_proj = Z_COLS // INPROJ_TN
    for j in range(INRNN_LEAD):
        project(j)

    la = la_ref[...]
    log2a_scale = (-LRU_C * LOG2E) * (jnp.maximum(-la, 0.0)
                                      + jnp.log1p(jnp.exp(-jnp.abs(la))))
    per = RNN_CB // LANES
    for ct in range(RNN_TILES):
        ch = slice(ct * RNN_CB, (ct + 1) * RNN_CB)
        xc = cb_ref[:, ch]
        for tap in range(CONV_WIDTH):
            off = hdr - (CONV_WIDTH - 1 - tap) * n_b
            x_tap = jnp.concatenate(
                [xt_ref[ct * per + s, off:off + rows, :] for s in range(per)], axis=1)
            xc = xc + cw_ref[tap:tap + 1, ch] * x_tap
        xcb = xc.astype(BF16)
        sg = []
        for gate in range(2):
            gcols = slice(gate * RNN_CB, (gate + 1) * RNN_CB)
            pre = jnp.dot(xcb, wg_ref[ct, :, gcols], preferred_element_type=F32)
            sg.append(_sigmoid(pre + bg_ref[ct, :, gcols]))
            project(INRNN_LEAD + ct * 2 + gate)
        a = jnp.exp2(log2a_scale[:, ch] * sg[0])
        y = 1.0 - a * a
        bb = (y * lax.rsqrt(jnp.maximum(y, SQRT_FLOOR))) * (sg[1] * xc)
        for s in range(per):
            a_ref[ct * per + s] = a[:, s * LANES:(s + 1) * LANES]
            b_ref[ct * per + s] = bb[:, s * LANES:(s + 1) * LANES]

    for s in range(RNN_SLABS):
        xt_ref[s, 0:hdr, :] = xt_ref[s, rows:rows + hdr, :]

    for j in range(INRNN_LEAD + RNN_TILES * 2, n_proj):
        project(j)

    hs = [h_ref[s] for s in range(RNN_SLABS)]
    for t in range(tb):
        for s in range(RNN_SLABS):
            hs[s] = a_ref[s, t * n_b:(t + 1) * n_b, :] * hs[s] + b_ref[s, t * n_b:(t + 1) * n_b, :]
            b_ref[s, t * n_b:(t + 1) * n_b, :] = hs[s]
    for s in range(RNN_SLABS):
        h_ref[s] = hs[s]

    for b in range(n_b):
        h = jnp.concatenate(
            [b_ref[s, pl.ds(b, tb, stride=n_b), :] for s in range(RNN_SLABS)], axis=1)
        zr = zr_ref[b * tb:(b + 1) * tb, :]
        yr_ref[b] = (h * (zr * _sigmoid(zr))).astype(BF16)


def _in_proj_rnn(x3, g, w, conv_w, conv_b, wg, bg, lru_a):
    n_b, seq, _ = x3.shape
    rows = n_b * INRNN_TB
    const = lambda *shape: pl.BlockSpec(shape, lambda i: (0,) * len(shape))
    return pl.pallas_call(
        _inproj_rnn_kernel,
        grid=(seq // INRNN_TB,),
        in_specs=[
            pl.BlockSpec((n_b, INRNN_TB, D_MODEL), lambda i: (0, i, 0)),
            const(1, D_MODEL),
            pl.BlockSpec((D_MODEL, D_IN_TOTAL), lambda i: (0, 0), pipeline_mode=pl.Buffered(1)),
            const(CONV_WIDTH, D_MODEL), const(1, D_MODEL),
            const(RNN_TILES, RNN_CB, 2 * RNN_CB), const(RNN_TILES, 1, 2 * RNN_CB),
            const(1, D_MODEL),
        ],
        out_specs=[
            pl.BlockSpec((n_b, INRNN_TB, Z_COLS), lambda i: (0, i, 0)),
            pl.BlockSpec((n_b, INRNN_TB, D_MODEL), lambda i: (0, i, 0)),
        ],
        out_shape=[
            jax.ShapeDtypeStruct((n_b, seq, Z_COLS), BF16),
            jax.ShapeDtypeStruct((n_b, seq, D_MODEL), BF16),
        ],
        scratch_shapes=[
            pltpu.VMEM((RNN_SLABS, (CONV_WIDTH - 1) * n_b + rows, LANES), F32),
            pltpu.VMEM((rows, D_MODEL), F32),
            pltpu.VMEM((RNN_SLABS, rows, LANES), F32),
            pltpu.VMEM((RNN_SLABS, rows, LANES), F32),
            pltpu.VMEM((RNN_SLABS, n_b, LANES), F32),
        ],
        compiler_params=pltpu.CompilerParams(
            dimension_semantics=("arbitrary",),
            vmem_limit_bytes=VMEM_LIMIT),
        name="in_proj_rnn",
    )(x3, g, w, conv_w, conv_b, wg, bg, lru_a)


ATT_TQ = 512
ATT_KC = 512
ATT_NC = 1
BF16_ROWS_PER_VREG = 16
ATT_VROWS = V_DIM + BF16_ROWS_PER_VREG
ATT_BOUND = 32.0


def _attn_kernel(lq1_ref, lk1_ref, lq2_ref, lk2_ref, sg_ref, q_ref, k_ref, v_ref, za_ref,
                 o_ref, q2_ref, vt_ref, bias_ref, s_ref, m_ref, acc_ref, *, lam_init):
    tq, kc, nc = ATT_TQ, ATT_KC, ATT_NC
    assert tq == nc * kc
    n_tiles = q_ref.shape[1] // tq
    n_chunks = v_ref.shape[1] // kc

    lam = (jnp.exp(jnp.sum(lq1_ref[...] * lk1_ref[...], axis=-1, keepdims=True))
           - jnp.exp(jnp.sum(lq2_ref[...] * lk2_ref[...], axis=-1, keepdims=True))
           + lam_init)

    @pl.when((pl.program_id(0) == 0) & (pl.program_id(1) == 0))
    def _():
        row = lax.broadcasted_iota(jnp.int32, (kc, 2 * tq), 0)
        col = lax.broadcasted_iota(jnp.int32, (kc, 2 * tq), 1)
        diff = row - jnp.where(col >= tq, col - tq, col)
        for u in range(nc):
            bias_ref[u] = jnp.where(diff <= -u * kc, 0.0, -jnp.inf)

    def max_norms2(x, lane, acc):
        sq = x.astype(F32)
        sq = sq * sq
        for half, keep in enumerate((lane < HEAD_DIM, lane >= HEAD_DIM)):
            n2 = jnp.max(jnp.sum(jnp.where(keep, sq, 0.0), axis=1, keepdims=True),
                         axis=0, keepdims=True)
            acc[half] = n2 if acc[half] is None else jnp.maximum(acc[half], n2)

    qlane = lax.broadcasted_iota(jnp.int32, (tq, 2 * HEAD_DIM), 1)
    qn2 = [None, None]
    for t in range(n_tiles):
        q = q_ref[0, t * tq:(t + 1) * tq, :]
        zero = jnp.zeros_like(q)
        q2_ref[t, 0:tq, :] = jnp.where(qlane < HEAD_DIM, q, zero)
        q2_ref[t, tq:2 * tq, :] = jnp.where(qlane >= HEAD_DIM, q, zero)
        max_norms2(q, qlane, qn2)

    ones_rows = jnp.where(
        lax.broadcasted_iota(jnp.int32, (ATT_VROWS - V_DIM, kc), 0) == 0, 1.0, 0.0
    ).astype(BF16)
    for c in range(n_chunks):
        vt_ref[c, 0:V_DIM, :] = v_ref[0, c * kc:(c + 1) * kc, :].T
        vt_ref[c, V_DIM:ATT_VROWS, :] = ones_rows

    klane = lax.broadcasted_iota(jnp.int32, (kc, 2 * HEAD_DIM), 1)
    kn2 = [None, None]
    for c in range(n_chunks):
        max_norms2(k_ref[0, c * kc:(c + 1) * kc, :], klane, kn2)
    bound2 = jnp.maximum(qn2[0] * kn2[0], qn2[1] * kn2[1])
    bounded = jnp.max(bound2) <= ATT_BOUND * ATT_BOUND

    def scores(t, c):
        if isinstance(c, int):
            kt = k_ref[0, c * kc:(c + 1) * kc, :]
        else:
            kt = k_ref[0, pl.ds(pl.multiple_of(c * kc, kc), kc), :]
        return lax.dot_general(kt, q2_ref[t], (((1,), (1,)), ((), ())),
                               preferred_element_type=F32)

    def finalize(t, acc):
        o2 = acc[0:V_DIM] * (1.0 / acc[V_DIM:V_DIM + 1])
        o = (o2[:, :tq] - lam * o2[:, tq:]).T
        o = o * lax.rsqrt(jnp.mean(o * o, axis=-1, keepdims=True) + NORM_EPS)
        o = o * sg_ref[...] * (1.0 - lam_init)
        start = t * tq if isinstance(t, int) else pl.multiple_of(t * tq, tq)
        za = za_ref[0, pl.ds(start, tq), :].astype(F32)
        o_ref[0, pl.ds(start, tq), :] = (o * (za * _sigmoid(za))).astype(BF16)

    for t in range(n_tiles):
        acc = None
        for c in range((t + 1) * nc):
            s = scores(t, c)
            if c >= t * nc:
                s = s + bias_ref[c - t * nc]
            pv = jnp.dot(vt_ref[c], jnp.exp2(s).astype(BF16),
                         preferred_element_type=F32)
            acc = pv if acc is None else acc + pv
        finalize(t, acc)

    @pl.when(jnp.logical_not(bounded))
    def _():
        def step(t, pair, first):
            for u in range(nc):
                s = scores(t, pair * nc + u)
                s_ref[u] = s + bias_ref[u] if first else s
            m_blk = functools.reduce(
                jnp.maximum, [jnp.max(s_ref[u], axis=0, keepdims=True) for u in range(nc)])
            if first:
                m_new = m_blk
            else:
                m_new = jnp.maximum(m_ref[...], m_blk)
                alpha = jnp.exp2(m_ref[...] - m_new)
            m_ref[...] = m_new
            pv = None
            for u in range(nc):
                d = jnp.dot(vt_ref[pair * nc + u], jnp.exp2(s_ref[u] - m_new).astype(BF16),
                            preferred_element_type=F32)
                pv = d if pv is None else pv + d
            acc_ref[...] = pv if first else alpha * acc_ref[...] + pv

        def tile_body(t, _):
            step(t, t, True)

            def rest(pair, _):
                step(t, pair, False)
                return 0
            lax.fori_loop(0, t, rest, 0)
            finalize(t, acc_ref[...])
            return 0
        lax.fori_loop(0, n_tiles, tile_body, 0)


def _attn_branch(z3, lq1, lk1, lq2, lk2, subln_g, lam_init):
    n_b, seq, _ = z3.shape
    small = lambda n: pl.BlockSpec((1, n), lambda b, h: (0, 0))
    head = lambda col: pl.BlockSpec((1, seq, LANES), lambda b, h: (b, 0, col - Z_COL0 + h))
    return pl.pallas_call(
        functools.partial(_attn_kernel, lam_init=lam_init),
        grid=(n_b, N_HEADS),
        in_specs=[
            small(HEAD_DIM), small(HEAD_DIM), small(HEAD_DIM), small(HEAD_DIM),
            small(V_DIM),
            head(COL_Q), head(COL_K), head(COL_V), head(COL_ZA),
        ],
        out_specs=pl.BlockSpec((1, seq, LANES), lambda b, h: (b, 0, h)),
        out_shape=jax.ShapeDtypeStruct((n_b, seq, N_HEADS * V_DIM), BF16),
        scratch_shapes=[
            pltpu.VMEM((seq // ATT_TQ, 2 * ATT_TQ, LANES), BF16),
            pltpu.VMEM((seq // ATT_KC, ATT_VROWS, ATT_KC), BF16),
            pltpu.VMEM((ATT_NC, ATT_KC, 2 * ATT_TQ), F32),
            pltpu.VMEM((ATT_NC, ATT_KC, 2 * ATT_TQ), F32),
            pltpu.VMEM((1, 2 * ATT_TQ), F32),
            pltpu.VMEM((ATT_VROWS, 2 * ATT_TQ), F32),
        ],
        compiler_params=pltpu.CompilerParams(
            dimension_semantics=("arbitrary", "arbitrary"),
            vmem_limit_bytes=VMEM_LIMIT),
        name="diff_attn",
    )(lq1, lk1, lq2, lk2, subln_g, z3, z3, z3, z3)


def _merge_kernel(x_ref, yr_ref, ya_ref, gm_ref, wr_ref, wa_ref, wo_ref, pg_ref, o_ref):
    pr = jnp.dot(yr_ref[...], wr_ref[...], preferred_element_type=F32)
    pa = jnp.dot(ya_ref[...], wa_ref[...], preferred_element_type=F32)
    g = _sigmoid(gm_ref[...].astype(F32))
    m = g[:, :D_MODEL] * pr + g[:, D_MODEL:] * pa
    y = jnp.dot(m.astype(BF16), wo_ref[...], preferred_element_type=F32)
    y = y * lax.rsqrt(jnp.mean(y * y, axis=-1, keepdims=True) + NORM_EPS)
    o_ref[...] = x_ref[...] + y * pg_ref[...]


MERGE_DEEP_BUFFERS = 3


def _merge(x2, yr, ya, z2, wr, wa, wo, post_g, tm=1024):
    n_tok = x2.shape[0]
    gm_block = (COL_GM - Z_COL0) * LANES // (2 * D_MODEL)
    rows = lambda w, **kw: pl.BlockSpec((tm, w), lambda i: (i, 0), **kw)

    def outer(x_hbm, yr_hbm, ya_hbm, z_hbm, wr_ref, wa_ref, wo_ref, pg_ref, o_hbm):
        def step(x_ref, yr_ref, ya_ref, gm_ref, o_ref):
            _merge_kernel(x_ref, yr_ref, ya_ref, gm_ref, wr_ref, wa_ref, wo_ref, pg_ref, o_ref)

        pltpu.emit_pipeline(
            step,
            grid=(n_tok // tm,),
            in_specs=[
                rows(D_MODEL, pipeline_mode=pl.Buffered(MERGE_DEEP_BUFFERS)),
                rows(D_MODEL), rows(D_MODEL),
                pl.BlockSpec((tm, 2 * D_MODEL), lambda i: (i, gm_block),
                             pipeline_mode=pl.Buffered(MERGE_DEEP_BUFFERS)),
            ],
            out_specs=[rows(D_MODEL)],
        )(x_hbm, yr_hbm, ya_hbm, z_hbm, o_hbm)

    hbm = pl.BlockSpec(memory_space=pl.ANY)
    vmem = pl.BlockSpec(memory_space=pltpu.VMEM)
    return pl.pallas_call(
        outer,
        in_specs=[hbm, hbm, hbm, hbm, vmem, vmem, vmem, vmem],
        out_specs=hbm,
        out_shape=jax.ShapeDtypeStruct((n_tok, D_MODEL), F32),
        compiler_params=pltpu.CompilerParams(vmem_limit_bytes=VMEM_LIMIT),
        name="merge",
    )(x2, yr, ya, z2, wr, wa, wo, post_g)


def _q_column_scale():
    col = np.ones((1, D_IN_TOTAL), np.float32)
    col[:, COL_Q * LANES:COL_K * LANES] = HEAD_DIM ** -0.5 * math.log2(math.e)
    return jnp.asarray(col)


def _block_diag_gate_weights(wa, wx):
    per = RNN_CB // RNN_BLOCK

    def expand(w):
        w = w.reshape(D_MODEL // RNN_CB, per, RNN_BLOCK, RNN_BLOCK)
        eye = jnp.eye(per, dtype=w.dtype)
        return jnp.einsum('cgij,gh->cgihj', w, eye).reshape(
            D_MODEL // RNN_CB, RNN_CB, RNN_CB)

    return jnp.concatenate([expand(wa), expand(wx)], axis=-1)


def kernel(x, pre_g, post_g, w_in, conv_w, conv_b, lru_wa, lru_ba, lru_wx, lru_bx, lru_a,
           attn_lq1, attn_lk1, attn_lq2, attn_lk2, subln_g, w_br_rnn, w_br_attn, w_out):
    n_b, seq, d = x.shape
    depth = pre_g.shape[0]
    h3 = x
    for l in range(depth):
        lam_init = 0.8 - 0.6 * math.exp(-0.3 * l)
        wg = _block_diag_gate_weights(lru_wa[l], lru_wx[l]).astype(BF16)
        bg = jnp.concatenate([lru_ba[l].reshape(RNN_TILES, 1, RNN_CB),
                              lru_bx[l].reshape(RNN_TILES, 1, RNN_CB)], axis=-1)
        z3, y_r = _in_proj_rnn(h3, pre_g[l][None, :],
                               (w_in[l] * _q_column_scale()).astype(BF16),
                               conv_w[l], conv_b[l][None, :], wg, bg, lru_a[l][None, :])

        y_a = _attn_branch(z3, attn_lq1[l][None, :], attn_lk1[l][None, :],
                           attn_lq2[l][None, :], attn_lk2[l][None, :],
                           subln_g[l][None, :], lam_init)

        h2 = _merge(h3.reshape(n_b * seq, d), y_r.reshape(n_b * seq, d),
                    y_a.reshape(n_b * seq, d), z3.reshape(n_b * seq, Z_COLS),
                    w_br_rnn[l].astype(BF16), w_br_attn[l].astype(BF16),
                    w_out[l].astype(BF16), post_g[l][None, :])
        h3 = h2.reshape(n_b, seq, d)
    return h3
```
